```python
import math
import jax, jax.numpy as jnp
from jax import lax
import numpy as np

D_MODEL = 1024
BATCH = 8
SEQ = 2048
DEPTH = 1
DEC_BATCH = 128
DEC_SEQ = 1
PAST_LEN = 16384
PAGE_SIZE = 128

MIX_WIDTH = D_MODEL
MLSTM_WIDTH = MIX_WIDTH // 2
POOL_WIDTH = MIX_WIDTH - MLSTM_WIDTH
N_HEADS = 4
HEAD_DIM = MLSTM_WIDTH // N_HEADS
POOL_WINDOWS = (2, 4, 8, 16)
N_POOL_GROUPS = len(POOL_WINDOWS)
POOL_GROUP_DIM = POOL_WIDTH // N_POOL_GROUPS
POOL_BUF = max(POOL_WINDOWS) - 1
D_FF = 4 * D_MODEL
CHUNK = 64
EPS = 1e-6
IN_COLS = 4 * MLSTM_WIDTH + 2 * N_HEADS + POOL_WIDTH

kernel_name = "hymba_mlstm_pool_adaln_step"


def _rmsnorm(x, g):
    xf = x.astype(jnp.float32)
    y = xf * lax.rsqrt(jnp.mean(xf * xf, axis=-1, keepdims=True) + EPS)
    return y * g.astype(jnp.float32)


def _mlstm_chunk(carry, xs):
    C0, n0, m0 = carry
    q, k, v, logf, ig = xs
    L = q.shape[2]
    b = jnp.cumsum(logf, axis=-1)
    causal = jnp.tril(jnp.ones((L, L), dtype=bool))
    dmat = b[..., :, None] - b[..., None, :] + ig[..., None, :]
    dmat = jnp.where(causal, dmat, -jnp.inf)
    g = b + m0[..., None]
    m = jnp.maximum(g, jnp.max(dmat, axis=-1))
    w = jnp.exp(dmat - m[..., None])
    a = jnp.exp(g - m)
    s = jnp.einsum('bhtd,bhsd->bhts', q, k) * w
    num = a[..., None] * jnp.einsum('bhtd,bhde->bhte', q, C0) + jnp.einsum('bhts,bhse->bhte', s, v)
    den = a * jnp.einsum('bhtd,bhd->bht', q, n0) + jnp.sum(s, axis=-1)
    h = num / jnp.maximum(jnp.abs(den), jnp.exp(-m))[..., None]
    mL = m[..., -1]
    aL = jnp.exp(g[..., -1] - mL)
    wL = jnp.exp(b[..., -1:] - b + ig - mL[..., None])
    C1 = aL[..., None, None] * C0 + jnp.einsum('bhs,bhsd,bhse->bhde', wL, k, v)
    n1 = aL[..., None] * n0 + jnp.einsum('bhs,bhsd->bhd', wL, k)
    return (C1, n1, mL), h


def _mlstm_sequence(q, k, v, logf, ig, C0, n0, m0):
    B, H, L, Dh = q.shape
    lc = CHUNK if L % CHUNK == 0 else L
    nc = L // lc

    def to_chunks(t):
        t = t.reshape(t.shape[:2] + (nc, lc) + t.shape[3:])
        return jnp.moveaxis(t, 2, 0)

    xs = (to_chunks(q), to_chunks(k), to_chunks(v), to_chunks(logf), to_chunks(ig))
    (C1, n1, m1), h = lax.scan(_mlstm_chunk, (C0, n0, m0), xs)
    h = jnp.moveaxis(h, 0, 2).reshape(B, H, L, Dh)
    return h, C1, n1, m1


def _pool_mixer(u, buf, pos0, w_pool, pool_scale):
    B, L, P = u.shape
    ext = jnp.concatenate([buf.astype(jnp.float32), u], axis=1)
    cs = jnp.cumsum(ext, axis=1)
    cs = jnp.pad(cs, ((0, 0), (1, 0), (0, 0)))
    pos = (pos0 + jnp.arange(L)).astype(jnp.float32)
    means = []
    for gi, win in enumerate(POOL_WINDOWS):
        sl = slice(gi * POOL_GROUP_DIM, (gi + 1) * POOL_GROUP_DIM)
        end = cs[:, POOL_BUF + 1:, sl]
        start = cs[:, POOL_BUF + 1 - win:POOL_BUF + 1 - win + L, sl]
        cnt = jnp.minimum(pos + 1.0, float(win))[None, :, None]
        means.append((end - start) / cnt)
    pooled = jnp.concatenate(means, axis=-1) - u
    pooled = pooled.reshape(B, L, N_POOL_GROUPS, POOL_GROUP_DIM)
    out = jnp.einsum('blgc,gcd->blgd', pooled, w_pool.astype(jnp.float32)).reshape(B, L, P)
    return out * pool_scale.astype(jnp.float32), ext[:, -POOL_BUF:]


def _layer(x, c, C0, n0, m0, buf, pos0, w_ada, b_ada, g_pre1, g_post1, w_in, b_ig, b_fg,
           g_head, w_pool, pool_scale, w_out, g_pre2, g_post2, w_up, w_down):
    dt = x.dtype
    B, L, _ = x.shape
    f32 = jnp.float32
    mod = jnp.einsum('bd,de->be', jax.nn.silu(c.astype(f32)), w_ada.astype(f32)) + b_ada.astype(f32)
    sh1, sc1, ga1, sh2, sc2, ga2 = jnp.split(mod[:, None, :], 6, axis=-1)

    hn = (_rmsnorm(x, g_pre1) * (1.0 + sc1) + sh1).astype(dt)
    z = jnp.einsum('bld,de->ble', hn, w_in).astype(f32)
    Wm = MLSTM_WIDTH
    q, k, v, o = z[..., :Wm], z[..., Wm:2 * Wm], z[..., 2 * Wm:3 * Wm], z[..., 3 * Wm:4 * Wm]
    ig = z[..., 4 * Wm:4 * Wm + N_HEADS] + b_ig.astype(f32)
    fg = z[..., 4 * Wm + N_HEADS:4 * Wm + 2 * N_HEADS] + b_fg.astype(f32)
    u = z[..., 4 * Wm + 2 * N_HEADS:]

    def heads(t):
        return t.reshape(B, L, N_HEADS, HEAD_DIM).transpose(0, 2, 1, 3)

    qh, kh, vh = heads(q), heads(k) * (HEAD_DIM ** -0.5), heads(v)
    logf = jax.nn.log_sigmoid(fg).transpose(0, 2, 1)
    igh = ig.transpose(0, 2, 1)
    h, C1, n1, m1 = _mlstm_sequence(qh, kh, vh, logf, igh, C0.astype(f32), n0.astype(f32), m0.astype(f32))
    h = h.transpose(0, 2, 1, 3)
    h = _rmsnorm(h, g_head) * jax.nn.sigmoid(heads(o).transpose(0, 2, 1, 3))
    h = h.reshape(B, L, Wm)

    p_out, buf1 = _pool_mixer(u, buf, pos0, w_pool, pool_scale)
    mix = jnp.concatenate([h, p_out], axis=-1).astype(dt)
    mix = jnp.einsum('blm,md->bld', mix, w_out)
    x = (x.astype(f32) + ga1 * _rmsnorm(mix, g_post1)).astype(dt)

    hn2 = (_rmsnorm(x, g_pre2) * (1.0 + sc2) + sh2).astype(dt)
    f = jnp.square(jax.nn.relu(jnp.einsum('bld,df->blf', hn2, w_up)))
    f = jnp.einsum('blf,fd->bld', f, w_down)
    x = (x.astype(f32) + ga2 * _rmsnorm(f, g_post2)).astype(dt)
    return x, C1, n1, m1, buf1


def setup_inputs(seed: int = 0) -> dict:
    key = jax.random.key(seed)
    ks = jax.random.split(key, 24)
    nrm = jax.random.normal
    f32 = jnp.float32
    d = {}
    d['x_prompt'] = nrm(ks[0], (BATCH, SEQ, D_MODEL), f32)
    d['x_sample'] = nrm(ks[1], (DEC_BATCH, DEC_SEQ, D_MODEL), f32)
    d['c_prompt'] = nrm(ks[2], (BATCH, D_MODEL), f32)
    d['c_sample'] = nrm(ks[3], (DEC_BATCH, D_MODEL), f32)
    d['state_C'] = nrm(ks[4], (DEPTH, DEC_BATCH, N_HEADS, HEAD_DIM, HEAD_DIM), f32) * HEAD_DIM ** -0.5
    d['state_n'] = nrm(ks[5], (DEPTH, DEC_BATCH, N_HEADS, HEAD_DIM), f32) * 0.5
    d['state_m'] = nrm(ks[6], (DEPTH, DEC_BATCH, N_HEADS), f32)
    d['state_pool'] = nrm(ks[7], (DEPTH, DEC_BATCH, POOL_BUF, POOL_WIDTH), f32)
    d['w_ada'] = nrm(ks[8], (DEPTH, D_MODEL, 6 * D_MODEL), f32) * D_MODEL ** -0.5
    d['b_ada'] = nrm(ks[9], (DEPTH, 6 * D_MODEL), f32) * 0.02
    d['g_pre1'] = 1.0 + 0.05 * nrm(ks[10], (DEPTH, D_MODEL), f32)
    d['g_post1'] = 1.0 + 0.05 * nrm(ks[11], (DEPTH, D_MODEL), f32)
    d['w_in'] = nrm(ks[12], (DEPTH, D_MODEL, IN_COLS), f32) * D_MODEL ** -0.5
    d['b_ig'] = 0.1 * nrm(ks[13], (DEPTH, N_HEADS), f32)
    d['b_fg'] = jnp.linspace(3.0, 6.0, N_HEADS, dtype=f32)[None, :] + 0.1 * nrm(ks[14], (DEPTH, N_HEADS), f32)
    d['g_head'] = 1.0 + 0.05 * nrm(ks[15], (DEPTH, HEAD_DIM), f32)
    d['w_pool'] = nrm(ks[16], (DEPTH, N_POOL_GROUPS, POOL_GROUP_DIM, POOL_GROUP_DIM), f32) * POOL_GROUP_DIM ** -0.5
    d['pool_scale'] = 0.5 + 0.1 * nrm(ks[17], (DEPTH, POOL_WIDTH), f32)
    d['w_out'] = nrm(ks[18], (DEPTH, MIX_WIDTH, D_MODEL), f32) * MIX_WIDTH ** -0.5
    d['g_pre2'] = 1.0 + 0.05 * nrm(ks[19], (DEPTH, D_MODEL), f32)
    d['g_post2'] = 1.0 + 0.05 * nrm(ks[20], (DEPTH, D_MODEL), f32)
    d['w_up'] = nrm(ks[21], (DEPTH, D_MODEL, D_FF), f32) * D_MODEL ** -0.5
    d['w_down'] = nrm(ks[22], (DEPTH, D_FF, D_MODEL), f32) * D_FF ** -0.5
    return d


def reference(x_prompt, x_sample, c_prompt, c_sample, state_C, state_n, state_m, state_pool,
              w_ada, b_ada, g_pre1, g_post1, w_in, b_ig, b_fg, g_head, w_pool, pool_scale,
              w_out, g_pre2, g_post2, w_up, w_down):
    f32 = jnp.float32
    xp, xs = x_prompt, x_sample
    Cp_l, np_l, mp_l, pp_l = [], [], [], []
    Cs_l, ns_l, ms_l, ps_l = [], [], [], []
    for l in range(DEPTH):
        w = (w_ada[l], b_ada[l], g_pre1[l], g_post1[l], w_in[l], b_ig[l], b_fg[l], g_head[l],
             w_pool[l], pool_scale[l], w_out[l], g_pre2[l], g_post2[l], w_up[l], w_down[l])
        C0 = jnp.zeros((BATCH, N_HEADS, HEAD_DIM, HEAD_DIM), f32)
        n0 = jnp.zeros((BATCH, N_HEADS, HEAD_DIM), f32)
        m0 = jnp.zeros((BATCH, N_HEADS), f32)
        b0 = jnp.zeros((BATCH, POOL_BUF, POOL_WIDTH), f32)
        xp, Cp, npv, mp, pp = _layer(xp, c_prompt, C0, n0, m0, b0, 0, *w)
        xs, Cs, nsv, ms, ps = _layer(xs, c_sample, state_C[l], state_n[l], state_m[l], state_pool[l], PAST_LEN, *w)
        Cp_l.append(Cp.astype(x_prompt.dtype)); np_l.append(npv.astype(x_prompt.dtype))
        mp_l.append(mp.astype(x_prompt.dtype)); pp_l.append(pp.astype(x_prompt.dtype))
        Cs_l.append(Cs.astype(state_C.dtype)); ns_l.append(nsv.astype(state_n.dtype))
        ms_l.append(ms.astype(state_m.dtype)); ps_l.append(ps.astype(state_pool.dtype))
    return (xp, xs,
            jnp.stack(Cp_l), jnp.stack(np_l), jnp.stack(mp_l), jnp.stack(pp_l),
            jnp.stack(Cs_l), jnp.stack(ns_l), jnp.stack(ms_l), jnp.stack(ps_l))
```

```python
import functools

import jax
import jax.numpy as jnp
from jax import lax
from jax.experimental import pallas as pl
from jax.experimental.pallas import tpu as pltpu

F32 = jnp.float32
BF16 = jnp.bfloat16

D_MODEL = 1024
N_HEADS = 4
HEAD_DIM = 128
MLSTM_WIDTH = N_HEADS * HEAD_DIM
POOL_WIDTH = 512
POOL_WINDOWS = (2, 4, 8, 16)
POOL_GROUP_DIM = 128
POOL_BUF = 15
POOL_HIST = 16
D_FF = 4 * D_MODEL
EPS = 1e-6
PAST_LEN = 16384
K_SCALE = HEAD_DIM ** -0.5

MIX_TOKENS = 256
MLP_ROWS = 512
FF_CHUNK = 1024
STATE_TOKENS = 8
VMEM_LIMIT = 56 * 1024 * 1024


def _dot(a, b):
    return jnp.dot(a, b, preferred_element_type=F32)


def _dot_nt(a, b):
    return lax.dot_general(a, b, (((1,), (1,)), ((), ())), preferred_element_type=F32)


def _dot_tn(a, b):
    return lax.dot_general(a, b, (((0,), (0,)), ((), ())), preferred_element_type=F32)


def _rms(x):
    return x * lax.rsqrt(jnp.mean(x * x, axis=-1, keepdims=True) + EPS)


def _ada_kernel(c_ref, w_ref, b_ref, o_ref):
    c = c_ref[...]
    s = c * jax.nn.sigmoid(c)
    o_ref[...] = _dot(s.astype(BF16), w_ref[...].astype(BF16)) + b_ref[...]


def _ada(c_all, w_ada, b_ada):
    n = c_all.shape[0]
    tn = 1024
    return pl.pallas_call(
        _ada_kernel,
        grid=(6 * D_MODEL // tn,),
        in_specs=[
            pl.BlockSpec((n, D_MODEL), lambda j: (0, 0)),
            pl.BlockSpec((D_MODEL, tn), lambda j: (0, j)),
            pl.BlockSpec((1, tn), lambda j: (0, j)),
        ],
        out_specs=pl.BlockSpec((n, tn), lambda j: (0, j)),
        out_shape=jax.ShapeDtypeStruct((n, 6 * D_MODEL), F32),
        compiler_params=pltpu.CompilerParams(
            dimension_semantics=("arbitrary",), vmem_limit_bytes=VMEM_LIMIT),
        name="ada_mod",
    )(c_all, w_ada, b_ada)


def _mix_kernel(x_ref, mod_ref, gpre_ref, gpost_ref, wtm_ref, wfm_ref, gb_ref, ghead_ref,
                wpool_ref, pscale_ref, wout_ref,
                x1_ref, c_ref, n_ref, m_ref, pool_ref,
                cn_s, m_s, hist_s):
    tl = MIX_TOKENS
    t = pl.program_id(1)
    nt = pl.num_programs(1)

    @pl.when(t == 0)
    def _():
        cn_s[...] = jnp.zeros_like(cn_s)
        m_s[...] = jnp.zeros_like(m_s)
        hist_s[...] = jnp.zeros_like(hist_s)

    x = x_ref[...]
    mod = mod_ref[...]
    sh1 = mod[:, 0:D_MODEL]
    sc1 = mod[:, D_MODEL:2 * D_MODEL]
    ga1 = mod[:, 2 * D_MODEL:3 * D_MODEL]
    hn = (_rms(x) * gpre_ref[...]) * (1.0 + sc1) + sh1
    hnb = hn.astype(BF16)
    ztm = _dot(hnb, wtm_ref[...])
    zfm = _dot_nt(wfm_ref[...], hnb)

    gt = zfm[MLSTM_WIDTH:MLSTM_WIDTH + 2 * N_HEADS, :] + gb_ref[...]
    ig = gt[0:N_HEADS, :]
    logf = jax.nn.log_sigmoid(gt[N_HEADS:2 * N_HEADS, :])
    lane = lax.broadcasted_iota(jnp.int32, (N_HEADS, tl), 1)
    bcs = logf
    k = 1
    while k < tl:
        bcs = bcs + jnp.where(lane >= k, pltpu.roll(bcs, k, axis=1), 0.0)
        k *= 2
    rr = ig - bcs

    row_i = lax.broadcasted_iota(jnp.int32, (tl, tl), 0)
    col_i = lax.broadcasted_iota(jnp.int32, (tl, tl), 1)
    causal = col_i <= row_i
    ones_col = (lax.broadcasted_iota(jnp.int32, (tl, HEAD_DIM), 1) == 0).astype(BF16)
    ghead = ghead_ref[...]

    heads = []
    for h in range(N_HEADS):
        sl = slice(h * HEAD_DIM, (h + 1) * HEAD_DIM)
        q = ztm[:, sl]
        v = ztm[:, MLSTM_WIDTH + h * HEAD_DIM:MLSTM_WIDTH + (h + 1) * HEAD_DIM]
        o = ztm[:, 2 * MLSTM_WIDTH + h * HEAD_DIM:2 * MLSTM_WIDTH + (h + 1) * HEAD_DIM]
        kt = zfm[sl, :] * K_SCALE
        r_row = rr[h:h + 1, :]
        lf_row = logf[h:h + 1, :]
        m0 = m_s[h]

        rmat = jnp.where(causal, r_row, -jnp.inf)
        mcol = jnp.maximum(jnp.max(rmat, axis=-1, keepdims=True), m0)
        bcol = jnp.sum(jnp.where(causal, lf_row, 0.0), axis=-1, keepdims=True)
        wmat = jnp.exp(rmat - mcol)
        a = jnp.exp(m0 - mcol)

        qb = q.astype(BF16)
        vb = v.astype(BF16)
        s = _dot(qb, kt.astype(BF16)) * wmat
        cn = cn_s[h]
        qcn = _dot(qb, cn.astype(BF16))
        num = a * qcn[:, 0:HEAD_DIM] + _dot(s.astype(BF16), vb)
        den = a * qcn[:, HEAD_DIM:HEAD_DIM + 1] + jnp.sum(s, axis=-1, keepdims=True)
        hh = num / jnp.maximum(jnp.abs(den), jnp.exp(-(bcol + mcol)))
        hh = (_rms(hh) * ghead) * jax.nn.sigmoid(o)
        heads.append(hh.astype(BF16))

        ml = mcol[tl - 1:tl, :]
        al = jnp.exp(m0 - ml)
        wl = jnp.exp(r_row - ml)
        kw = (kt * wl).astype(BF16)
        vext = jnp.concatenate([vb, ones_col], axis=1)
        cn_s[h] = al * cn + _dot(kw, vext)
        m_s[h] = bcs[h:h + 1, tl - 1:tl] + ml

    u = ztm[:, 3 * MLSTM_WIDTH:3 * MLSTM_WIDTH + POOL_WIDTH]
    ext = jnp.concatenate([hist_s[...], u], axis=0)
    g = POOL_GROUP_DIM
    p2 = ext + pltpu.roll(ext, 1, axis=0)
    p4 = p2[:, g:] + pltpu.roll(p2[:, g:], 2, axis=0)
    p8 = p4[:, g:] + pltpu.roll(p4[:, g:], 4, axis=0)
    p16 = p8[:, g:] + pltpu.roll(p8[:, g:], 8, axis=0)
    wsum = (p2[:, 0:g], p4[:, 0:g], p8[:, 0:g], p16)
    pos = (t * tl + lax.broadcasted_iota(jnp.int32, (tl, 1), 0)).astype(F32)
    pouts = []
    for gi, win in enumerate(POOL_WINDOWS):
        cnt = jnp.minimum(pos + 1.0, float(win))
        ug = u[:, gi * g:(gi + 1) * g]
        pooled = wsum[gi][POOL_HIST:, :] / cnt - ug
        pouts.append(_dot(pooled.astype(BF16), wpool_ref[gi]))
    p_out = jnp.concatenate(pouts, axis=1) * pscale_ref[...]
    hist_s[...] = ext[tl:, :]

    mixh = jnp.concatenate(heads, axis=1)
    mix = _dot(mixh, wout_ref[0:MLSTM_WIDTH, :]) + _dot(p_out.astype(BF16), wout_ref[MLSTM_WIDTH:, :])
    x1_ref[...] = x + ga1 * (_rms(mix) * gpost_ref[...])

    @pl.when(t == nt - 1)
    def _():
        for h in range(N_HEADS):
            c_ref[h] = cn_s[h][:, 0:HEAD_DIM]
            n_ref[h] = cn_s[h][:, HEAD_DIM:HEAD_DIM + 1]
            m_ref[h] = m_s[h]
        pool_ref[...] = ext[tl + POOL_HIST - POOL_BUF:, :]


def _mix_prompt(x, mod_p, g_pre1, g_post1, w_tm, w_fm, gate_bias, g_head, w_pool, pool_scale, w_out):
    bsz, seq, _ = x.shape
    tl = MIX_TOKENS
    nt = seq // tl
    const2 = lambda b, t: (0, 0)
    const3 = lambda b, t: (0, 0, 0)
    return pl.pallas_call(
        _mix_kernel,
        grid=(bsz, nt),
        in_specs=[
            pl.BlockSpec((None, tl, D_MODEL), lambda b, t: (b, t, 0)),
            pl.BlockSpec((None, 1, 6 * D_MODEL), lambda b, t: (b, 0, 0)),
            pl.BlockSpec((1, D_MODEL), const2),
            pl.BlockSpec((1, D_MODEL), const2),
            pl.BlockSpec(w_tm.shape, const2),
            pl.BlockSpec(w_fm.shape, const2),
            pl.BlockSpec((2 * N_HEADS, 1), const2),
            pl.BlockSpec((1, HEAD_DIM), const2),
            pl.BlockSpec(w_pool.shape, const3),
            pl.BlockSpec((1, POOL_WIDTH), const2),
            pl.BlockSpec(w_out.shape, const2),
        ],
        out_specs=[
            pl.BlockSpec((None, tl, D_MODEL), lambda b, t: (b, t, 0)),
            pl.BlockSpec((None, N_HEADS, HEAD_DIM, HEAD_DIM), lambda b, t: (b, 0, 0, 0)),
            pl.BlockSpec((None, N_HEADS, HEAD_DIM, 1), lambda b, t: (b, 0, 0, 0)),
            pl.BlockSpec((None, N_HEADS, 1, 1), lambda b, t: (b, 0, 0, 0)),
            pl.BlockSpec((None, POOL_BUF, POOL_WIDTH), lambda b, t: (b, 0, 0)),
        ],
        out_shape=[
            jax.ShapeDtypeStruct((bsz, seq, D_MODEL), F32),
            jax.ShapeDtypeStruct((bsz, N_HEADS, HEAD_DIM, HEAD_DIM), F32),
            jax.ShapeDtypeStruct((bsz, N_HEADS, HEAD_DIM, 1), F32),
            jax.ShapeDtypeStruct((bsz, N_HEADS, 1, 1), F32),
            jax.ShapeDtypeStruct((bsz, POOL_BUF, POOL_WIDTH), F32),
        ],
        scratch_shapes=[
            pltpu.VMEM((N_HEADS, HEAD_DIM, 2 * HEAD_DIM), F32),
            pltpu.VMEM((N_HEADS, 1, 1), F32),
            pltpu.VMEM((POOL_HIST, POOL_WIDTH), F32),
        ],
        compiler_params=pltpu.CompilerParams(
            dimension_semantics=("arbitrary", "arbitrary"), vmem_limit_bytes=VMEM_LIMIT),
        name="mix_prompt",
    )(x, mod_p, g_pre1, g_post1, w_tm, w_fm, gate_bias, g_head, w_pool, pool_scale, w_out)


def _mlp_kernel(x_ref, mod_ref, gpre_ref, gpost_ref, wup_ref, wdn_ref, o_ref):
    x = x_ref[...]
    mod = mod_ref[...]
    sh2 = mod[:, 3 * D_MODEL:4 * D_MODEL]
    sc2 = mod[:, 4 * D_MODEL:5 * D_MODEL]
    ga2 = mod[:, 5 * D_MODEL:6 * D_MODEL]
    hn = (_rms(x) * gpre_ref[...]) * (1.0 + sc2) + sh2
    hb = hn.astype(BF16)
    acc = jnp.zeros(x.shape, F32)
    for j in range(D_FF // FF_CHUNK):
        f = _dot(hb, wup_ref[:, j * FF_CHUNK:(j + 1) * FF_CHUNK])
        f = jnp.square(jnp.maximum(f, 0.0))
        acc = acc + _dot(f.astype(BF16), wdn_ref[j * FF_CHUNK:(j + 1) * FF_CHUNK, :])
    o_ref[...] = x + ga2 * (_rms(acc) * gpost_ref[...])


def _mlp(x, mod, rows_per_mod, tm, g_pre2, g_post2, w_up, w_dn):
    n = x.shape[0]
    steps_per_mod = rows_per_mod // tm
    r = mod.shape[1]
    const2 = lambda i: (0, 0)
    return pl.pallas_call(
        _mlp_kernel,
        grid=(n // tm,),
        in_specs=[
            pl.BlockSpec((tm, D_MODEL), lambda i: (i, 0)),
            pl.BlockSpec((None, r, 6 * D_MODEL), lambda i: (i // steps_per_mod, 0, 0)),
            pl.BlockSpec((1, D_MODEL), const2),
            pl.BlockSpec((1, D_MODEL), const2),
            pl.BlockSpec(w_up.shape, const2, pipeline_mode=pl.Buffered(1)),
            pl.BlockSpec(w_dn.shape, const2, pipeline_mode=pl.Buffered(1)),
        ],
        out_specs=pl.BlockSpec((tm, D_MODEL), lambda i: (i, 0)),
        out_shape=jax.ShapeDtypeStruct((n, D_MODEL), F32),
        compiler_params=pltpu.CompilerParams(
            dimension_semantics=("arbitrary",), vmem_limit_bytes=VMEM_LIMIT),
        name="mlp",
    )(x, mod, g_pre2, g_post2, w_up, w_dn)


def _sproj_kernel(x_ref, mod_ref, gpre_ref, w_ref, z_ref):
    x = x_ref[...]
    mod = mod_ref[...]
    sh1 = mod[:, 0:D_MODEL]
    sc1 = mod[:, D_MODEL:2 * D_MODEL]
    hn = (_rms(x) * gpre_ref[...]) * (1.0 + sc1) + sh1
    z_ref[...] = _dot(hn.astype(BF16), w_ref[...])


def _sample_proj(x, mod_s, g_pre1, w_s):
    n = x.shape[0]
    return pl.pallas_call(
        _sproj_kernel,
        out_shape=jax.ShapeDtypeStruct((n, w_s.shape[1]), F32),
        compiler_params=pltpu.CompilerParams(vmem_limit_bytes=VMEM_LIMIT),
        name="sample_proj",
    )(x, mod_s, g_pre1, w_s)


def _sample_gates(gates, m0):
    ig = gates[:, 0:N_HEADS]
    logf = jax.nn.log_sigmoid(gates[:, N_HEADS:2 * N_HEADS])
    g = logf + m0
    m = jnp.maximum(g, ig)
    return m, jnp.exp(ig - m), jnp.exp(g - m)


_SQ, _SK, _SV, _SO, _SU, _SG = 0, 512, 1024, 1536, 2048, 2560


def _sstate_kernel(z_ref, m0_ref, gb_ref, c0_ref, c1_ref, qc_ref):
    z = z_ref[...]
    gates = z[:, _SG:_SG + 2 * N_HEADS] + gb_ref[...]
    _, w, a = _sample_gates(gates, m0_ref[...])
    sub = lax.broadcasted_iota(jnp.int32, (8, HEAD_DIM), 0)
    for j in range(STATE_TOKENS):
        rows = []
        for h in range(N_HEADS):
            q = z[j:j + 1, _SQ + h * HEAD_DIM:_SQ + (h + 1) * HEAD_DIM]
            k = z[j:j + 1, _SK + h * HEAD_DIM:_SK + (h + 1) * HEAD_DIM] * K_SCALE
            v = z[j:j + 1, _SV + h * HEAD_DIM:_SV + (h + 1) * HEAD_DIM]
            ah = a[j:j + 1, h:h + 1]
            wh = w[j:j + 1, h:h + 1]
            c0 = c0_ref[j, h]
            q8 = jnp.broadcast_to(q, (8, HEAD_DIM)).astype(BF16)
            rows.append(_dot(q8, c0.astype(BF16))[0:1, :])
            kw8 = jnp.where(sub == 0, jnp.broadcast_to(k * wh, (8, HEAD_DIM)), 0.0).astype(BF16)
            v8 = jnp.broadcast_to(v, (8, HEAD_DIM)).astype(BF16)
            c1_ref[j, h] = ah * c0 + _dot_tn(kw8, v8)
        qc_ref[j:j + 1, :] = jnp.concatenate(rows, axis=1)


def _sample_state(z_s, m0, gate_bias_row, c0):
    n = z_s.shape[0]
    bt = STATE_TOKENS
    return pl.pallas_call(
        _sstate_kernel,
        grid=(n // bt,),
        in_specs=[
            pl.BlockSpec((bt, z_s.shape[1]), lambda i: (i, 0)),
            pl.BlockSpec((bt, N_HEADS), lambda i: (i, 0)),
            pl.BlockSpec((1, 2 * N_HEADS), lambda i: (0, 0)),
            pl.BlockSpec((bt, N_HEADS, HEAD_DIM, HEAD_DIM), lambda i: (i, 0, 0, 0)),
        ],
        out_specs=[
            pl.BlockSpec((bt, N_HEADS, HEAD_DIM, HEAD_DIM), lambda i: (i, 0, 0, 0)),
            pl.BlockSpec((bt, MLSTM_WIDTH), lambda i: (i, 0)),
        ],
        out_shape=[
            jax.ShapeDtypeStruct(c0.shape, F32),
            jax.ShapeDtypeStruct((n, MLSTM_WIDTH), F32),
        ],
        compiler_params=pltpu.CompilerParams(
            dimension_semantics=("arbitrary",), vmem_limit_bytes=VMEM_LIMIT),
        name="sample_state",
    )(z_s, m0, gate_bias_row, c0)


def _smix_kernel(x_ref, z_ref, qc_ref, mod_ref, n0_ref, m0_ref, pool_ref, gb_ref, ghead_ref,
                 wpool_ref, pscale_ref, wout_ref, gpost_ref,
                 x1_ref, n1_ref, m1_ref, poolo_ref):
    x = x_ref[...]
    z = z_ref[...]
    mod = mod_ref[...]
    ga1 = mod[:, 2 * D_MODEL:3 * D_MODEL]
    gates = z[:, _SG:_SG + 2 * N_HEADS] + gb_ref[...]
    m, w, a = _sample_gates(gates, m0_ref[...])
    m1_ref[...] = m
    ghead = ghead_ref[...]
    heads = []
    for h in range(N_HEADS):
        sl = slice(h * HEAD_DIM, (h + 1) * HEAD_DIM)
        q = z[:, _SQ + h * HEAD_DIM:_SQ + (h + 1) * HEAD_DIM]
        k = z[:, _SK + h * HEAD_DIM:_SK + (h + 1) * HEAD_DIM] * K_SCALE
        v = z[:, _SV + h * HEAD_DIM:_SV + (h + 1) * HEAD_DIM]
        o = z[:, _SO + h * HEAD_DIM:_SO + (h + 1) * HEAD_DIM]
        n0 = n0_ref[:, sl]
        ah = a[:, h:h + 1]
        wh = w[:, h:h + 1]
        s = jnp.sum(q * k, axis=-1, keepdims=True) * wh
        num = ah * qc_ref[:, sl] + s * v
        den = ah * jnp.sum(q * n0, axis=-1, keepdims=True) + s
        hh = num / jnp.maximum(jnp.abs(den), jnp.exp(-m[:, h:h + 1]))
        n1_ref[:, sl] = ah * n0 + wh * k
        hh = (_rms(hh) * ghead) * jax.nn.sigmoid(o)
        heads.append(hh.astype(BF16))

    u = z[:, _SU:_SU + POOL_WIDTH]
    g = POOL_GROUP_DIM
    pouts = []
    for gi, win in enumerate(POOL_WINDOWS):
        ug = u[:, gi * g:(gi + 1) * g]
        wsum = ug
        for r in range(POOL_BUF - (win - 1), POOL_BUF):
            wsum = wsum + pool_ref[:, r * POOL_WIDTH + gi * g:r * POOL_WIDTH + (gi + 1) * g]
        cnt = min(PAST_LEN + 1.0, float(win))
        pooled = wsum / cnt - ug
        pouts.append(_dot(pooled.astype(BF16), wpool_ref[gi]))
    p_out = jnp.concatenate(pouts, axis=1) * pscale_ref[...]
    poolo_ref[:, 0:(POOL_BUF - 1) * POOL_WIDTH] = pool_ref[:, POOL_WIDTH:]
    poolo_ref[:, (POOL_BUF - 1) * POOL_WIDTH:] = u

    mixh = jnp.concatenate(heads, axis=1)
    mix = _dot(mixh, wout_ref[0:MLSTM_WIDTH, :]) + _dot(p_out.astype(BF16), wout_ref[MLSTM_WIDTH:, :])
    x1_ref[...] = x + ga1 * (_rms(mix) * gpost_ref[...])


def _sample_mix(x, z_s, qc, mod_s, n0, m0, pool2d, gate_bias_row, g_head, w_pool, pool_scale, w_out, g_post1):
    n = x.shape[0]
    return pl.pallas_call(
        _smix_kernel,
        out_shape=[
            jax.ShapeDtypeStruct((n, D_MODEL), F32),
            jax.ShapeDtypeStruct((n, MLSTM_WIDTH), F32),
            jax.ShapeDtypeStruct((n, N_HEADS), F32),
            jax.ShapeDtypeStruct(pool2d.shape, F32),
        ],
        compiler_params=pltpu.CompilerParams(vmem_limit_bytes=VMEM_LIMIT),
        name="sample_mix",
    )(x, z_s, qc, mod_s, n0, m0, pool2d, gate_bias_row, g_head, w_pool, pool_scale, w_out, g_post1)


def kernel(x_prompt, x_sample, c_prompt, c_sample, state_C, state_n, state_m, state_pool, w_ada, b_ada,
           g_pre1, g_post1, w_in, b_ig, b_fg, g_head, w_pool, pool_scale, w_out, g_pre2, g_post2,
           w_up, w_down):
    depth = w_ada.shape[0]
    assert depth == 1, "single-layer step"
    bsz, seq, _ = x_prompt.shape
    nb = x_sample.shape[0]
    assert x_sample.shape[1] == 1 and seq % MIX_TOKENS == 0 and nb % STATE_TOKENS == 0
    l = 0

    wi = w_in[l]
    wq, wk, wv, wo = (wi[:, i * MLSTM_WIDTH:(i + 1) * MLSTM_WIDTH] for i in range(4))
    wg = wi[:, 4 * MLSTM_WIDTH:4 * MLSTM_WIDTH + 2 * N_HEADS]
    wu = wi[:, 4 * MLSTM_WIDTH + 2 * N_HEADS:]
    w_tm = jnp.concatenate([wq, wv, wo, wu], axis=1).astype(BF16)
    w_fm = jnp.concatenate([wk.T, wg.T], axis=0).astype(BF16)
    w_s = jnp.concatenate([wq, wk, wv, wo, wu, wg], axis=1).astype(BF16)
    w_out_b = w_out[l].astype(BF16)
    w_pool_b = w_pool[l].astype(BF16)
    w_up_b = w_up[l].astype(BF16)
    w_dn_b = w_down[l].astype(BF16)
    gate_bias = jnp.concatenate([b_ig[l], b_fg[l]])
    gpre1 = g_pre1[l][None, :]
    gpost1 = g_post1[l][None, :]
    gpre2 = g_pre2[l][None, :]
    gpost2 = g_post2[l][None, :]
    ghead = g_head[l][None, :]
    pscale = pool_scale[l][None, :]

    c_all = jnp.concatenate([c_prompt, c_sample], axis=0)
    mod = _ada(c_all, w_ada[l], b_ada[l][None, :])
    mod_p = mod[0:bsz].reshape(bsz, 1, 6 * D_MODEL)
    mod_s = mod[bsz:]

    x1p, c_p, n_p, m_p, pool_p = _mix_prompt(
        x_prompt, mod_p, gpre1, gpost1, w_tm, w_fm, gate_bias[:, None], ghead, w_pool_b, pscale, w_out_b)
    y_p = _mlp(x1p.reshape(bsz * seq, D_MODEL), mod_p, seq, MLP_ROWS, gpre2, gpost2, w_up_b, w_dn_b)
    y_p = y_p.reshape(bsz, seq, D_MODEL)

    xs = x_sample.reshape(nb, D_MODEL)
    m0 = state_m[l]
    z_s = _sample_proj(xs, mod_s, gpre1, w_s)
    c_s, qc = _sample_state(z_s, m0, gate_bias[None, :], state_C[l])
    x1s, n_s, m_s, pool_s = _sample_mix(
        xs, z_s, qc, mod_s, state_n[l].reshape(nb, MLSTM_WIDTH), m0,
        state_pool[l].reshape(nb, POOL_BUF * POOL_WIDTH), gate_bias[None, :], ghead, w_pool_b, pscale,
        w_out_b, gpost1)
    y_s = _mlp(x1s, mod_s.reshape(1, nb, 6 * D_MODEL), nb, nb, gpre2, gpost2, w_up_b, w_dn_b)

    return (y_p, y_s.reshape(nb, 1, D_MODEL),
            c_p[None], n_p.reshape(1, bsz, N_HEADS, HEAD_DIM), m_p.reshape(1, bsz, N_HEADS), pool_p[None],
            c_s[None], n_s.reshape(1, nb, N_HEADS, HEAD_DIM), m_s[None],
            pool_s.reshape(1, nb, POOL_BUF, POOL_WIDTH))
```

```python
import jax
import jax.numpy as jnp
from jax import lax
from jax.experimental import pallas as pl
from jax.experimental.pallas import tpu as pltpu

F32 = jnp.float32
BF16 = jnp.bfloat16

D_MODEL = 1024
N_HEADS = 4
HEAD_DIM = 128
MLSTM_WIDTH = N_HEADS * HEAD_DIM
POOL_WIDTH = 512
POOL_WINDOWS = (2, 4, 8, 16)
POOL_GROUP_DIM = 128
POOL_BUF = 15
POOL_HIST = 16
D_FF = 4 * D_MODEL
EPS = 1e-6
PAST_LEN = 16384
K_SCALE = HEAD_DIM ** -0.5
GATE_ROWS = 16

_CQ, _CK, _CV, _CO = 0, 512, 1024, 1536
_CG = 2048
_CU = _CG + 2 * N_HEADS
IN_COLS = _CU + POOL_WIDTH
TAIL_COLS = IN_COLS - _CG

MIX_TOKENS = 512
CHUNK = 256
MLP_ROWS = 512
FF_CHUNK = 1024
STATE_TOKENS = 8
VMEM_LIMIT = 56 * 1024 * 1024


def _dot(a, b):
    return jnp.dot(a, b, preferred_element_type=F32)


def _dot_nt(a, b):
    return lax.dot_general(a, b, (((1,), (1,)), ((), ())), preferred_element_type=F32)


def _dot_tn(a, b):
    return lax.dot_general(a, b, (((0,), (0,)), ((), ())), preferred_element_type=F32)


def _rms(x):
    return x * lax.rsqrt(jnp.mean(x * x, axis=-1, keepdims=True) + EPS)


def _stream_cast(srcs, stage, sem, sink):
    def copy(i):
        return pltpu.make_async_copy(srcs[i], stage.at[i % 2], sem.at[i % 2])

    copy(0).start()
    for i in range(len(srcs)):
        if i + 1 < len(srcs):
            copy(i + 1).start()
        copy(i).wait()
        sink(i, stage[i % 2])


def _ada_kernel(c_ref, w_ref, b_ref, o_ref):
    c = c_ref[...]
    s = c * jax.nn.sigmoid(c)
    o_ref[...] = _dot(s.astype(BF16), w_ref[...].astype(BF16)) + b_ref[...]


def _ada(c_all, w_ada, b_ada):
    n = c_all.shape[0]
    tn = 1024
    return pl.pallas_call(
        _ada_kernel,
        grid=(6 * D_MODEL // tn,),
        in_specs=[
            pl.BlockSpec((n, D_MODEL), lambda j: (0, 0)),
            pl.BlockSpec((D_MODEL, tn), lambda j: (0, j)),
            pl.BlockSpec((1, tn), lambda j: (0, j)),
        ],
        out_specs=pl.BlockSpec((n, tn), lambda j: (0, j)),
        out_shape=jax.ShapeDtypeStruct((n, 6 * D_MODEL), F32),
        compiler_params=pltpu.CompilerParams(
            dimension_semantics=("arbitrary",), vmem_limit_bytes=VMEM_LIMIT),
        name="ada_mod",
    )(c_all, w_ada, b_ada)


def _cumsum_lanes(x, upper):
    hi = x.astype(BF16)
    r1 = x - hi.astype(F32)
    mid = r1.astype(BF16)
    lo = (r1 - mid.astype(F32)).astype(BF16)
    y = _dot(jnp.concatenate([hi, mid, lo], axis=0), upper)
    n = x.shape[0]
    return y[0:n] + y[n:2 * n] + y[2 * n:3 * n]


def _mix_kernel(x_ref, mod_ref, gpre_ref, gpost_ref, win_hbm, wtail_ref, gb_ref, ghead_ref,
                wpool_ref, pscale_ref, wout_hbm,
                x1_ref, c_ref, n_ref, m_ref, pool_ref,
                wtm_s, wkt_s, wgt_s, wout_s, stage, sem, cn_s, m_s, hist_s):
    tl = MIX_TOKENS
    lc = CHUNK
    b = pl.program_id(0)
    t = pl.program_id(1)
    nt = pl.num_programs(1)

    @pl.when((b == 0) & (t == 0))
    def _():
        half = D_MODEL // 2
        srcs = [win_hbm.at[0, :, pl.ds(c, MLSTM_WIDTH)] for c in (_CQ, _CK, _CV, _CO)]
        srcs += [wout_hbm.at[0, :, pl.ds(c, half)] for c in (0, half)]

        def sink(i, blk):
            if i == 0:
                wtm_s[:, 0:MLSTM_WIDTH] = blk.astype(BF16)
            elif i == 1:
                wkt_s[...] = blk.T.astype(BF16)
            elif i == 2:
                wtm_s[:, MLSTM_WIDTH:2 * MLSTM_WIDTH] = blk.astype(BF16)
            elif i == 3:
                wtm_s[:, 2 * MLSTM_WIDTH:3 * MLSTM_WIDTH] = blk.astype(BF16)
            else:
                wout_s[:, (i - 4) * half:(i - 3) * half] = blk.astype(BF16)

        _stream_cast(srcs, stage, sem, sink)
        tail = wtail_ref[...]
        wtm_s[:, 3 * MLSTM_WIDTH:] = tail[:, 2 * N_HEADS:].astype(BF16)
        gt16 = tail[:, 0:128].T[0:GATE_ROWS, :]
        rows = lax.broadcasted_iota(jnp.int32, (GATE_ROWS, D_MODEL), 0)
        wgt_s[...] = jnp.where(rows < 2 * N_HEADS, gt16, 0.0).astype(BF16)

    @pl.when(t == 0)
    def _():
        cn_s[...] = jnp.zeros_like(cn_s)
        m_s[...] = jnp.zeros_like(m_s)
        hist_s[...] = jnp.zeros_like(hist_s)

    x = x_ref[...]
    mod = mod_ref[...]
    sh1 = mod[:, 0:D_MODEL]
    sc1 = mod[:, D_MODEL:2 * D_MODEL]
    ga1 = mod[:, 2 * D_MODEL:3 * D_MODEL]
    hn = (_rms(x) * gpre_ref[...]) * (1.0 + sc1) + sh1
    hnb = hn.astype(BF16)
    gt = _dot_nt(wgt_s[...], hnb) + gb_ref[...]
    ztm = _dot(hnb, wtm_s[...])
    kt_all = _dot_nt(wkt_s[...], hnb)

    row_i = lax.broadcasted_iota(jnp.int32, (lc, lc), 0)
    col_i = lax.broadcasted_iota(jnp.int32, (lc, lc), 1)
    causal = col_i <= row_i
    upper = (row_i <= col_i).astype(BF16)
    ones_col = (lax.broadcasted_iota(jnp.int32, (lc, HEAD_DIM), 1) == 0).astype(BF16)
    ghead = ghead_ref[...]
    logf_all = jax.nn.log_sigmoid(gt)

    cns = [cn_s[h] for h in range(N_HEADS)]
    ms = [m_s[h] for h in range(N_HEADS)]
    head_rows = [[] for _ in range(N_HEADS)]
    for c in range(tl // lc):
        tok = slice(c * lc, (c + 1) * lc)
        ig = gt[0:N_HEADS, tok]
        logf = logf_all[N_HEADS:2 * N_HEADS, tok]
        bcs = _cumsum_lanes(logf_all[:, tok], upper)[N_HEADS:2 * N_HEADS]
        rr = ig - bcs
        for h in range(N_HEADS):
            sl = slice(h * HEAD_DIM, (h + 1) * HEAD_DIM)
            q = ztm[tok, sl]
            v = ztm[tok, MLSTM_WIDTH + h * HEAD_DIM:MLSTM_WIDTH + (h + 1) * HEAD_DIM]
            o = ztm[tok, 2 * MLSTM_WIDTH + h * HEAD_DIM:2 * MLSTM_WIDTH + (h + 1) * HEAD_DIM]
            kt = kt_all[sl, tok] * K_SCALE
            r_row = rr[h:h + 1, :]
            lf_row = logf[h:h + 1, :]
            m0 = ms[h]
            cn = cns[h]

            rmat = jnp.where(causal, r_row, -jnp.inf)
            mcol = jnp.maximum(jnp.max(rmat, axis=-1, keepdims=True), m0)
            bcol = jnp.sum(jnp.where(causal, lf_row, 0.0), axis=-1, keepdims=True)
            wmat = jnp.exp(rmat - mcol)
            a = jnp.exp(m0 - mcol)

            qb = q.astype(BF16)
            vb = v.astype(BF16)
            s = _dot(qb, kt.astype(BF16)) * wmat
            qcn = _dot(qb, cn.astype(BF16))
            num = a * qcn[:, 0:HEAD_DIM] + _dot(s.astype(BF16), vb)
            den = a * qcn[:, HEAD_DIM:HEAD_DIM + 1] + jnp.sum(s, axis=-1, keepdims=True)
            hh = num / jnp.maximum(jnp.abs(den), jnp.exp(-(bcol + mcol)))
            hh = (_rms(hh) * ghead) * jax.nn.sigmoid(o)
            head_rows[h].append(hh.astype(BF16))

            ml = mcol[lc - 1:lc, :]
            al = jnp.exp(m0 - ml)
            wl = jnp.exp(r_row - ml)
            kw = (kt * wl).astype(BF16)
            vext = jnp.concatenate([vb, ones_col], axis=1)
            cns[h] = al * cn + _dot(kw, vext)
            ms[h] = bcs[h:h + 1, lc - 1:lc] + ml
    for h in range(N_HEADS):
        cn_s[h] = cns[h]
        m_s[h] = ms[h]

    u = ztm[:, 3 * MLSTM_WIDTH:3 * MLSTM_WIDTH + POOL_WIDTH]
    ext = jnp.concatenate([hist_s[...], u], axis=0)
    g = POOL_GROUP_DIM
    p2 = ext + pltpu.roll(ext, 1, axis=0)
    p4 = p2[:, g:] + pltpu.roll(p2[:, g:], 2, axis=0)
    p8 = p4[:, g:] + pltpu.roll(p4[:, g:], 4, axis=0)
    p16 = p8[:, g:] + pltpu.roll(p8[:, g:], 8, axis=0)
    wsum = (p2[:, 0:g], p4[:, 0:g], p8[:, 0:g], p16)
    pos = (t * tl + lax.broadcasted_iota(jnp.int32, (tl, 1), 0)).astype(F32)
    pouts = []
    for gi, win in enumerate(POOL_WINDOWS):
        cnt = jnp.minimum(pos + 1.0, float(win))
        ug = u[:, gi * g:(gi + 1) * g]
        pooled = wsum[gi][POOL_HIST:, :] / cnt - ug
        pouts.append(_dot(pooled.astype(BF16), wpool_ref[gi].astype(BF16)))
    p_out = jnp.concatenate(pouts, axis=1) * pscale_ref[...]
    hist_s[...] = ext[tl:, :]

    mixh = jnp.concatenate([jnp.concatenate(r, axis=0) for r in head_rows], axis=1)
    mix = _dot(mixh, wout_s[0:MLSTM_WIDTH, :]) + _dot(p_out.astype(BF16), wout_s[MLSTM_WIDTH:, :])
    x1_ref[...] = x + ga1 * (_rms(mix) * gpost_ref[...])

    @pl.when(t == nt - 1)
    def _():
        for h in range(N_HEADS):
            c_ref[h] = cns[h][:, 0:HEAD_DIM]
            n_ref[h] = cns[h][:, HEAD_DIM:HEAD_DIM + 1]
            m_ref[h] = ms[h]
        pool_ref[...] = ext[tl + POOL_HIST - POOL_BUF:, :]


def _mix_prompt(x, mod_p, g_pre1, g_post1, w_in, w_tail, gate_bias, g_head, w_pool, pool_scale, w_out):
    bsz, seq, _ = x.shape
    tl = MIX_TOKENS
    nt = seq // tl
    const2 = lambda b, t: (0, 0)
    const3 = lambda b, t: (0, 0, 0)
    once = pl.Buffered(1)
    return pl.pallas_call(
        _mix_kernel,
        grid=(bsz, nt),
        in_specs=[
            pl.BlockSpec((None, tl, D_MODEL), lambda b, t: (b, t, 0)),
            pl.BlockSpec((None, 1, 6 * D_MODEL), lambda b, t: (b, 0, 0)),
            pl.BlockSpec((1, D_MODEL), const2),
            pl.BlockSpec((1, D_MODEL), const2),
            pl.BlockSpec(memory_space=pl.ANY),
            pl.BlockSpec(w_tail.shape, const2, pipeline_mode=once),
            pl.BlockSpec((GATE_ROWS, 1), const2),
            pl.BlockSpec((1, HEAD_DIM), const2),
            pl.BlockSpec(w_pool.shape, const3),
            pl.BlockSpec((1, POOL_WIDTH), const2),
            pl.BlockSpec(memory_space=pl.ANY),
        ],
        out_specs=[
            pl.BlockSpec((None, tl, D_MODEL), lambda b, t: (b, t, 0)),
            pl.BlockSpec((None, N_HEADS, HEAD_DIM, HEAD_DIM), lambda b, t: (b, 0, 0, 0)),
            pl.BlockSpec((None, N_HEADS, HEAD_DIM, 1), lambda b, t: (b, 0, 0, 0)),
            pl.BlockSpec((None, N_HEADS, 1, 1), lambda b, t: (b, 0, 0, 0)),
            pl.BlockSpec((None, POOL_BUF, POOL_WIDTH), lambda b, t: (b, 0, 0)),
        ],
        out_shape=[
            jax.ShapeDtypeStruct((bsz, seq, D_MODEL), F32),
            jax.ShapeDtypeStruct((bsz, N_HEADS, HEAD_DIM, HEAD_DIM), F32),
            jax.ShapeDtypeStruct((bsz, N_HEADS, HEAD_DIM, 1), F32),
            jax.ShapeDtypeStruct((bsz, N_HEADS, 1, 1), F32),
            jax.ShapeDtypeStruct((bsz, POOL_BUF, POOL_WIDTH), F32),
        ],
        scratch_shapes=[
            pltpu.VMEM((D_MODEL, 3 * MLSTM_WIDTH + POOL_WIDTH), BF16),
            pltpu.VMEM((MLSTM_WIDTH, D_MODEL), BF16),
            pltpu.VMEM((GATE_ROWS, D_MODEL), BF16),
            pltpu.VMEM((D_MODEL, D_MODEL), BF16),
            pltpu.VMEM((2, D_MODEL, MLSTM_WIDTH), F32),
            pltpu.SemaphoreType.DMA((2,)),
            pltpu.VMEM((N_HEADS, HEAD_DIM, 2 * HEAD_DIM), F32),
            pltpu.VMEM((N_HEADS, 1, 1), F32),
            pltpu.VMEM((POOL_HIST, POOL_WIDTH), F32),
        ],
        compiler_params=pltpu.CompilerParams(
            dimension_semantics=("arbitrary", "arbitrary"), vmem_limit_bytes=VMEM_LIMIT),
        name="mix_prompt",
    )(x, mod_p, g_pre1, g_post1, w_in, w_tail, gate_bias, g_head, w_pool, pool_scale, w_out)


def _mlp_rows(x, mod, gpre, gpost, wup_s, wdn_s):
    sh2 = mod[:, 3 * D_MODEL:4 * D_MODEL]
    sc2 = mod[:, 4 * D_MODEL:5 * D_MODEL]
    ga2 = mod[:, 5 * D_MODEL:6 * D_MODEL]
    hn = (_rms(x) * gpre) * (1.0 + sc2) + sh2
    hb = hn.astype(BF16)
    acc = jnp.zeros(x.shape, F32)
    for j in range(D_FF // FF_CHUNK):
        f = _dot(hb, wup_s[:, j * FF_CHUNK:(j + 1) * FF_CHUNK])
        f = jnp.square(jnp.maximum(f, 0.0))
        acc = acc + _dot(f.astype(BF16), wdn_s[j * FF_CHUNK:(j + 1) * FF_CHUNK, :])
    return x + ga2 * (_rms(acc) * gpost)


def _mlp_kernel(xp_ref, xs_ref, modp_ref, mods_ref, gpre_ref, gpost_ref, wup_hbm, wdn_hbm,
                yp_ref, ys_ref, wup_s, wdn_s, stage_u, stage_d, sem):
    i = pl.program_id(0)
    n_prompt = pl.num_programs(0) - 1

    @pl.when(i == 0)
    def _():
        ru = stage_u.shape[1]
        rd = stage_d.shape[1]

        def sink_u(k, blk):
            wup_s[k * ru:(k + 1) * ru, :] = blk.astype(BF16)

        def sink_d(k, blk):
            wdn_s[k * rd:(k + 1) * rd, :] = blk.astype(BF16)

        _stream_cast([wup_hbm.at[0, pl.ds(k * ru, ru), :] for k in range(D_MODEL // ru)], stage_u, sem, sink_u)
        _stream_cast([wdn_hbm.at[0, pl.ds(k * rd, rd), :] for k in range(D_FF // rd)], stage_d, sem, sink_d)

    @pl.when(i < n_prompt)
    def _():
        yp_ref[...] = _mlp_rows(xp_ref[...], modp_ref[...], gpre_ref[...], gpost_ref[...], wup_s, wdn_s)

    @pl.when(i == n_prompt)
    def _():
        ys_ref[...] = _mlp_rows(xs_ref[...], mods_ref[...], gpre_ref[...], gpost_ref[...], wup_s, wdn_s)


def _mlp(xp, xs, mod_p, mod_s, rows_per_mod, g_pre2, g_post2, w_up, w_dn):
    n_p = xp.shape[0]
    n_s = xs.shape[0]
    tm = MLP_ROWS
    n_tiles = n_p // tm
    steps_per_mod = rows_per_mod // tm
    last = n_tiles - 1
    const2 = lambda i: (0, 0)
    ptile = lambda i: (jnp.minimum(i, last), 0)
    return pl.pallas_call(
        _mlp_kernel,
        grid=(n_tiles + 1,),
        in_specs=[
            pl.BlockSpec((tm, D_MODEL), ptile),
            pl.BlockSpec((n_s, D_MODEL), const2),
            pl.BlockSpec((None, 1, 6 * D_MODEL), lambda i: (jnp.minimum(i, last) // steps_per_mod, 0, 0)),
            pl.BlockSpec((n_s, 6 * D_MODEL), const2),
            pl.BlockSpec((1, D_MODEL), const2),
            pl.BlockSpec((1, D_MODEL), const2),
            pl.BlockSpec(memory_space=pl.ANY),
            pl.BlockSpec(memory_space=pl.ANY),
        ],
        out_specs=[
            pl.BlockSpec((tm, D_MODEL), ptile),
            pl.BlockSpec((n_s, D_MODEL), const2),
        ],
        out_shape=[
            jax.ShapeDtypeStruct((n_p, D_MODEL), F32),
            jax.ShapeDtypeStruct((n_s, D_MODEL), F32),
        ],
        scratch_shapes=[
            pltpu.VMEM((D_MODEL, D_FF), BF16),
            pltpu.VMEM((D_FF, D_MODEL), BF16),
            pltpu.VMEM((2, 128, D_FF), F32),
            pltpu.VMEM((2, 512, D_MODEL), F32),
            pltpu.SemaphoreType.DMA((2,)),
        ],
        compiler_params=pltpu.CompilerParams(
            dimension_semantics=("arbitrary",), vmem_limit_bytes=VMEM_LIMIT),
        name="mlp",
    )(xp, xs, mod_p, mod_s, g_pre2, g_post2, w_up, w_dn)


def _sproj_kernel(x_ref, mod_ref, gpre_ref, w_ref, z_ref):
    x = x_ref[...]
    mod = mod_ref[...]
    sh1 = mod[:, 0:D_MODEL]
    sc1 = mod[:, D_MODEL:2 * D_MODEL]
    hn = (_rms(x) * gpre_ref[...]) * (1.0 + sc1) + sh1
    z_ref[...] = _dot(hn.astype(BF16), w_ref[...].astype(BF16))


def _sample_proj(x, mod_s, g_pre1, w_in2d):
    n = x.shape[0]
    return pl.pallas_call(
        _sproj_kernel,
        out_shape=jax.ShapeDtypeStruct((n, IN_COLS), F32),
        compiler_params=pltpu.CompilerParams(vmem_limit_bytes=VMEM_LIMIT),
        name="sample_proj",
    )(x, mod_s, g_pre1, w_in2d)


def _sample_gates(gates, m0):
    ig = gates[:, 0:N_HEADS]
    logf = jax.nn.log_sigmoid(gates[:, N_HEADS:2 * N_HEADS])
    g = logf + m0
    m = jnp.maximum(g, ig)
    return m, jnp.exp(ig - m), jnp.exp(g - m)


def _sstate_kernel(z_ref, m0_ref, gb_ref, c0_ref, c1_ref, qc_ref):
    z = z_ref[...]
    gates = z[:, _CG:_CG + 2 * N_HEADS] + gb_ref[...]
    _, w, a = _sample_gates(gates, m0_ref[...])
    sub = lax.broadcasted_iota(jnp.int32, (8, HEAD_DIM), 0)
    for j in range(STATE_TOKENS):
        rows = []
        for h in range(N_HEADS):
            q = z[j:j + 1, _CQ + h * HEAD_DIM:_CQ + (h + 1) * HEAD_DIM]
            k = z[j:j + 1, _CK + h * HEAD_DIM:_CK + (h + 1) * HEAD_DIM] * K_SCALE
            v = z[j:j + 1, _CV + h * HEAD_DIM:_CV + (h + 1) * HEAD_DIM]
            ah = a[j:j + 1, h:h + 1]
            wh = w[j:j + 1, h:h + 1]
            c0 = c0_ref[j, h]
            q8 = jnp.broadcast_to(q, (8, HEAD_DIM)).astype(BF16)
            rows.append(_dot(q8, c0.astype(BF16))[0:1, :])
            kw8 = jnp.where(sub == 0, jnp.broadcast_to(k * wh, (8, HEAD_DIM)), 0.0).astype(BF16)
            v8 = jnp.broadcast_to(v, (8, HEAD_DIM)).astype(BF16)
            c1_ref[j, h] = ah * c0 + _dot_tn(kw8, v8)
        qc_ref[j:j + 1, :] = jnp.concatenate(rows, axis=1)


def _sample_state(z_s, m0, gate_bias_row, c0):
    n = z_s.shape[0]
    bt = STATE_TOKENS
    return pl.pallas_call(
        _sstate_kernel,
        grid=(n // bt,),
        in_specs=[
            pl.BlockSpec((bt, z_s.shape[1]), lambda i: (i, 0)),
            pl.BlockSpec((bt, N_HEADS), lambda i: (i, 0)),
            pl.BlockSpec((1, 2 * N_HEADS), lambda i: (0, 0)),
            pl.BlockSpec((bt, N_HEADS, HEAD_DIM, HEAD_DIM), lambda i: (i, 0, 0, 0)),
        ],
        out_specs=[
            pl.BlockSpec((bt, N_HEADS, HEAD_DIM, HEAD_DIM), lambda i: (i, 0, 0, 0)),
            pl.BlockSpec((bt, MLSTM_WIDTH), lambda i: (i, 0)),
        ],
        out_shape=[
            jax.ShapeDtypeStruct(c0.shape, F32),
            jax.ShapeDtypeStruct((n, MLSTM_WIDTH), F32),
        ],
        compiler_params=pltpu.CompilerParams(
            dimension_semantics=("arbitrary",), vmem_limit_bytes=VMEM_LIMIT),
        name="sample_state",
    )(z_s, m0, gate_bias_row, c0)


def _smix_kernel(x_ref, z_ref, qc_ref, mod_ref, n0_ref, m0_ref, pool_ref, gb_ref, ghead_ref,
                 wpool_ref, pscale_ref, wout_ref, gpost_ref,
                 x1_ref, n1_ref, m1_ref, poolo_ref):
    x = x_ref[...]
    z = z_ref[...]
    mod = mod_ref[...]
    ga1 = mod[:, 2 * D_MODEL:3 * D_MODEL]
    gates = z[:, _CG:_CG + 2 * N_HEADS] + gb_ref[...]
    m, w, a = _sample_gates(gates, m0_ref[...])
    m1_ref[...] = m
    ghead = ghead_ref[...]
    heads = []
    for h in range(N_HEADS):
        sl = slice(h * HEAD_DIM, (h + 1) * HEAD_DIM)
        q = z[:, _CQ + h * HEAD_DIM:_CQ + (h + 1) * HEAD_DIM]
        k = z[:, _CK + h * HEAD_DIM:_CK + (h + 1) * HEAD_DIM] * K_SCALE
        v = z[:, _CV + h * HEAD_DIM:_CV + (h + 1) * HEAD_DIM]
        o = z[:, _CO + h * HEAD_DIM:_CO + (h + 1) * HEAD_DIM]
        n0 = n0_ref[:, sl]
        ah = a[:, h:h + 1]
        wh = w[:, h:h + 1]
        s = jnp.sum(q * k, axis=-1, keepdims=True) * wh
        num = ah * qc_ref[:, sl] + s * v
        den = ah * jnp.sum(q * n0, axis=-1, keepdims=True) + s
        hh = num / jnp.maximum(jnp.abs(den), jnp.exp(-m[:, h:h + 1]))
        n1_ref[:, sl] = ah * n0 + wh * k
        hh = (_rms(hh) * ghead) * jax.nn.sigmoid(o)
        heads.append(hh.astype(BF16))

    u = z[:, _CU:_CU + POOL_WIDTH]
    g = POOL_GROUP_DIM
    pouts = []
    for gi, win in enumerate(POOL_WINDOWS):
        ug = u[:, gi * g:(gi + 1) * g]
        wsum = ug
        for r in range(POOL_BUF - (win - 1), POOL_BUF):
            wsum = wsum + pool_ref[:, r * POOL_WIDTH + gi * g:r * POOL_WIDTH + (gi + 1) * g]
        cnt = min(PAST_LEN + 1.0, float(win))
        pooled = wsum / cnt - ug
        pouts.append(_dot(pooled.astype(BF16), wpool_ref[gi].astype(BF16)))
    p_out = jnp.concatenate(pouts, axis=1) * pscale_ref[...]
    poolo_ref[:, 0:(POOL_BUF - 1) * POOL_WIDTH] = pool_ref[:, POOL_WIDTH:]
    poolo_ref[:, (POOL_BUF - 1) * POOL_WIDTH:] = u

    mixh = jnp.concatenate(heads, axis=1)
    mix = (_dot(mixh, wout_ref[0:MLSTM_WIDTH, :].astype(BF16))
           + _dot(p_out.astype(BF16), wout_ref[MLSTM_WIDTH:, :].astype(BF16)))
    x1_ref[...] = x + ga1 * (_rms(mix) * gpost_ref[...])


def _sample_mix(x, z_s, qc, mod_s, n0, m0, pool2d, gate_bias_row, g_head, w_pool, pool_scale, w_out, g_post1):
    n = x.shape[0]
    return pl.pallas_call(
        _smix_kernel,
        out_shape=[
            jax.ShapeDtypeStruct((n, D_MODEL), F32),
            jax.ShapeDtypeStruct((n, MLSTM_WIDTH), F32),
            jax.ShapeDtypeStruct((n, N_HEADS), F32),
            jax.ShapeDtypeStruct(pool2d.shape, F32),
        ],
        compiler_params=pltpu.CompilerParams(vmem_limit_bytes=VMEM_LIMIT),
        name="sample_mix",
    )(x, z_s, qc, mod_s, n0, m0, pool2d, gate_bias_row, g_head, w_pool, pool_scale, w_out, g_post1)


def kernel(x_prompt, x_sample, c_prompt, c_sample, state_C, state_n, state_m, state_pool, w_ada, b_ada,
           g_pre1, g_post1, w_in, b_ig, b_fg, g_head, w_pool, pool_scale, w_out, g_pre2, g_post2,
           w_up, w_down):
    depth = w_ada.shape[0]
    assert depth == 1, "single-layer step"
    bsz, seq, _ = x_prompt.shape
    nb = x_sample.shape[0]
    assert x_sample.shape[1] == 1 and seq % MIX_TOKENS == 0 and nb % STATE_TOKENS == 0
    assert w_in.shape[2] == IN_COLS
    l = 0

    gate_bias = jnp.concatenate([b_ig[l], b_fg[l]])
    gate_bias_col = jnp.pad(gate_bias, (0, GATE_ROWS - 2 * N_HEADS))[:, None]
    gate_bias_row = gate_bias[None, :]
    gpre1 = g_pre1[l][None, :]
    gpost1 = g_post1[l][None, :]
    gpre2 = g_pre2[l][None, :]
    gpost2 = g_post2[l][None, :]
    ghead = g_head[l][None, :]
    pscale = pool_scale[l][None, :]
    w_tail = w_in[l][:, _CG:]

    c_all = jnp.concatenate([c_prompt, c_sample], axis=0)
    mod = _ada(c_all, w_ada[l], b_ada[l][None, :])
    mod_p = mod[0:bsz].reshape(bsz, 1, 6 * D_MODEL)
    mod_s = mod[bsz:]

    x1p, c_p, n_p, m_p, pool_p = _mix_prompt(
        x_prompt, mod_p, gpre1, gpost1, w_in, w_tail, gate_bias_col, ghead, w_pool[l], pscale, w_out)

    xs = x_sample.reshape(nb, D_MODEL)
    m0 = state_m[l]
    z_s = _sample_proj(xs, mod_s, gpre1, w_in[l])
    c_s, qc = _sample_state(z_s, m0, gate_bias_row, state_C[l])
    x1s, n_s, m_s, pool_s = _sample_mix(
        xs, z_s, qc, mod_s, state_n[l].reshape(nb, MLSTM_WIDTH), m0,
        state_pool[l].reshape(nb, POOL_BUF * POOL_WIDTH), gate_bias_row, ghead, w_pool[l], pscale,
        w_out[l], gpost1)

    y_p, y_s = _mlp(x1p.reshape(bsz * seq, D_MODEL), x1s, mod_p, mod_s, seq, gpre2, gpost2, w_up, w_down)

    return (y_p.reshape(bsz, seq, D_MODEL), y_s.reshape(nb, 1, D_MODEL),
            c_p[None], n_p.reshape(1, bsz, N_HEADS, HEAD_DIM), m_p.reshape(1, bsz, N_HEADS), pool_p[None],
            c_s[None], n_s.reshape(1, nb, N_HEADS, HEAD_DIM), m_s[None],
            pool_s.reshape(1, nb, POOL_BUF, POOL_WIDTH))
```

```python
import jax
import jax.numpy as jnp
from jax import lax
from jax.experimental import pallas as pl
from jax.experimental.pallas import tpu as pltpu

F32 = jnp.float32
BF16 = jnp.bfloat16

D_MODEL = 1024
N_HEADS = 4
HEAD_DIM = 128
MLSTM_WIDTH = N_HEADS * HEAD_DIM
POOL_WIDTH = 512
POOL_WINDOWS = (2, 4, 8, 16)
POOL_GROUP_DIM = 128
POOL_BUF = 15
POOL_HIST = 16
D_FF = 4 * D_MODEL
EPS = 1e-6
PAST_LEN = 16384
K_SCALE = HEAD_DIM ** -0.5
GATE_ROWS = 16

_CQ, _CK, _CV, _CO = 0, 512, 1024, 1536
_CG = 2048
_CU = _CG + 2 * N_HEADS
IN_COLS = _CU + POOL_WIDTH
TAIL_COLS = IN_COLS - _CG

MIX_TOKENS = 512
CHUNK = 256
MLP_ROWS = 512
FF_CHUNK = 1024
STATE_TOKENS = 8
VMEM_LIMIT = 56 * 1024 * 1024


def _dot(a, b):
    return jnp.dot(a, b, preferred_element_type=F32)


def _dot_nt(a, b):
    return lax.dot_general(a, b, (((1,), (1,)), ((), ())), preferred_element_type=F32)


def _dot_tn(a, b):
    return lax.dot_general(a, b, (((0,), (0,)), ((), ())), preferred_element_type=F32)


def _rms(x):
    return x * lax.rsqrt(jnp.mean(x * x, axis=-1, keepdims=True) + EPS)


def _stream_cast(srcs, stage, sem, sink):
    def copy(i):
        return pltpu.make_async_copy(srcs[i], stage.at[i % 2], sem.at[i % 2])

    copy(0).start()
    for i in range(len(srcs)):
        if i + 1 < len(srcs):
            copy(i + 1).start()
        copy(i).wait()
        sink(i, stage[i % 2])


def _ada_kernel(cp_ref, cs_ref, w_ref, b_ref, op_ref, os_ref):
    c = jnp.concatenate([cp_ref[...], cs_ref[...]], axis=0)
    s = c * jax.nn.sigmoid(c)
    mod = _dot(s.astype(BF16), w_ref[...].astype(BF16)) + b_ref[...]
    n_p = cp_ref.shape[0]
    op_ref[...] = mod[0:n_p]
    os_ref[...] = mod[n_p:]


def _ada(c_p, c_s, w_ada, b_ada):
    n_p = c_p.shape[0]
    n_s = c_s.shape[0]
    tn = 1024
    return pl.pallas_call(
        _ada_kernel,
        grid=(6 * D_MODEL // tn,),
        in_specs=[
            pl.BlockSpec((n_p, D_MODEL), lambda j: (0, 0)),
            pl.BlockSpec((n_s, D_MODEL), lambda j: (0, 0)),
            pl.BlockSpec((D_MODEL, tn), lambda j: (0, j)),
            pl.BlockSpec((1, tn), lambda j: (0, j)),
        ],
        out_specs=[
            pl.BlockSpec((n_p, tn), lambda j: (0, j)),
            pl.BlockSpec((n_s, tn), lambda j: (0, j)),
        ],
        out_shape=[
            jax.ShapeDtypeStruct((n_p, 6 * D_MODEL), F32),
            jax.ShapeDtypeStruct((n_s, 6 * D_MODEL), F32),
        ],
        compiler_params=pltpu.CompilerParams(
            dimension_semantics=("arbitrary",), vmem_limit_bytes=VMEM_LIMIT),
        name="ada_mod",
    )(c_p, c_s, w_ada, b_ada)


def _cumsum_lanes(x, upper):
    hi = x.astype(BF16)
    r1 = x - hi.astype(F32)
    mid = r1.astype(BF16)
    lo = (r1 - mid.astype(F32)).astype(BF16)
    y = _dot(jnp.concatenate([hi, mid, lo], axis=0), upper)
    n = x.shape[0]
    return y[0:n] + y[n:2 * n] + y[2 * n:3 * n]


def _mix_kernel(x_ref, mod_ref, gpre_ref, gpost_ref, winT_hbm, wgate_ref, gb_ref, ghead_ref,
                wpool_ref, pscale_ref, wout_hbm,
                x1_ref, c_ref, n_ref, m_ref, pool_ref,
                wtm_s, wkt_s, wgt_s, wout_s, stage, sem, cn_s, m_s, hist_s):
    tl = MIX_TOKENS
    lc = CHUNK
    b = pl.program_id(0)
    t = pl.program_id(1)
    nt = pl.num_programs(1)

    @pl.when((b == 0) & (t == 0))
    def _():
        half = D_MODEL // 2
        srcs = [winT_hbm.at[pl.ds(c, MLSTM_WIDTH), :] for c in (_CQ, _CK, _CV, _CO, _CU)]
        srcs += [wout_hbm.at[0, pl.ds(r, half), :] for r in (0, half)]
        tm_col = {0: 0, 2: MLSTM_WIDTH, 3: 2 * MLSTM_WIDTH, 4: 3 * MLSTM_WIDTH}

        def sink(i, blk):
            if i == 1:
                wkt_s[...] = blk.astype(BF16)
            elif i in tm_col:
                wtm_s[:, tm_col[i]:tm_col[i] + MLSTM_WIDTH] = blk.T.astype(BF16)
            else:
                wout_s[(i - 5) * half:(i - 4) * half, :] = blk.astype(BF16)

        _stream_cast(srcs, stage, sem, sink)
        pad = jnp.zeros((GATE_ROWS - 2 * N_HEADS, D_MODEL), F32)
        wgt_s[...] = jnp.concatenate([wgate_ref[...], pad], axis=0).astype(BF16)

    @pl.when(t == 0)
    def _():
        cn_s[...] = jnp.zeros_like(cn_s)
        m_s[...] = jnp.zeros_like(m_s)
        hist_s[...] = jnp.zeros_like(hist_s)

    x = x_ref[...]
    mod = mod_ref[...]
    sh1 = mod[:, 0:D_MODEL]
    sc1 = mod[:, D_MODEL:2 * D_MODEL]
    ga1 = mod[:, 2 * D_MODEL:3 * D_MODEL]
    hn = (_rms(x) * gpre_ref[...]) * (1.0 + sc1) + sh1
    hnb = hn.astype(BF16)
    gt = _dot_nt(wgt_s[...], hnb) + gb_ref[...]
    logf_all = jax.nn.log_sigmoid(gt)
    row_i = lax.broadcasted_iota(jnp.int32, (lc, lc), 0)
    col_i = lax.broadcasted_iota(jnp.int32, (lc, lc), 1)
    causal = col_i <= row_i
    upper = (row_i <= col_i).astype(BF16)
    ones_blk = jnp.ones((lc, HEAD_DIM), BF16)
    ghead = ghead_ref[...]
    n_chunks = tl // lc
    lane_tiles = lc // HEAD_DIM
    bcs_rows = []
    bcs_cols = []
    for c in range(n_chunks):
        b16 = _cumsum_lanes(logf_all[:, c * lc:(c + 1) * lc], upper)
        bcs_rows.append(b16[N_HEADS:2 * N_HEADS])
        bcs_cols.append(b16.T)

    ztm = _dot(hnb, wtm_s[...])
    kt_all = _dot_nt(wkt_s[...], hnb)

    cns = [cn_s[h] for h in range(N_HEADS)]
    ms = [m_s[h] for h in range(N_HEADS)]
    head_rows = [[] for _ in range(N_HEADS)]
    for c in range(n_chunks):
        tok = slice(c * lc, (c + 1) * lc)
        bcs = bcs_rows[c]
        rr = gt[0:N_HEADS, tok] - bcs
        for h in range(N_HEADS):
            sl = slice(h * HEAD_DIM, (h + 1) * HEAD_DIM)
            q = ztm[tok, sl]
            v = ztm[tok, MLSTM_WIDTH + h * HEAD_DIM:MLSTM_WIDTH + (h + 1) * HEAD_DIM]
            o = ztm[tok, 2 * MLSTM_WIDTH + h * HEAD_DIM:2 * MLSTM_WIDTH + (h + 1) * HEAD_DIM]
            kt = kt_all[sl, tok] * K_SCALE
            r_row = rr[h:h + 1, :]
            m0 = ms[h]
            cn = cns[h]

            rmat = jnp.where(causal, r_row, -jnp.inf)
            mcol = jnp.maximum(jnp.max(rmat, axis=-1, keepdims=True), m0)
            mcol_b = jnp.broadcast_to(mcol, (lc, HEAD_DIM))
            wmat = jnp.exp(rmat - jnp.concatenate([mcol_b] * lane_tiles, axis=1))
            a_b = jnp.exp(m0 - mcol_b)
            bcol = bcs_cols[c][:, N_HEADS + h:N_HEADS + h + 1]

            qb = q.astype(BF16)
            vext = jnp.concatenate([v.astype(BF16), ones_blk], axis=1)
            s = _dot(qb, kt.astype(BF16)) * wmat
            qcn = _dot(qb, cn.astype(BF16))
            sv = _dot(s.astype(BF16), vext)
            nd = jnp.concatenate([a_b, a_b], axis=1) * qcn + sv
            num = nd[:, 0:HEAD_DIM]
            den = nd[:, HEAD_DIM:]
            hh = num / jnp.maximum(jnp.abs(den), jnp.exp(-(bcol + mcol)))
            hh = (_rms(hh) * ghead) * jax.nn.sigmoid(o)
            head_rows[h].append(hh.astype(BF16))

            ml = mcol[lc - 1:lc, :]
            al = jnp.exp(m0 - ml)
            wl = jnp.exp(r_row - ml)
            kw = (kt * wl).astype(BF16)
            cns[h] = al * cn + _dot(kw, vext)
            ms[h] = bcs[h:h + 1, lc - 1:lc] + ml
    for h in range(N_HEADS):
        cn_s[h] = cns[h]
        m_s[h] = ms[h]

    u = ztm[:, 3 * MLSTM_WIDTH:3 * MLSTM_WIDTH + POOL_WIDTH]
    ext = jnp.concatenate([hist_s[...], u], axis=0)
    g = POOL_GROUP_DIM
    p2 = ext + pltpu.roll(ext, 1, axis=0)
    p4 = p2[:, g:] + pltpu.roll(p2[:, g:], 2, axis=0)
    p8 = p4[:, g:] + pltpu.roll(p4[:, g:], 4, axis=0)
    p16 = p8[:, g:] + pltpu.roll(p8[:, g:], 8, axis=0)
    wsum = (p2[:, 0:g], p4[:, 0:g], p8[:, 0:g], p16)
    pos = (t * tl + lax.broadcasted_iota(jnp.int32, (tl, 1), 0)).astype(F32)
    pouts = []
    for gi, win in enumerate(POOL_WINDOWS):
        cnt = jnp.minimum(pos + 1.0, float(win))
        ug = u[:, gi * g:(gi + 1) * g]
        pooled = wsum[gi][POOL_HIST:, :] / cnt - ug
        pouts.append(_dot(pooled.astype(BF16), wpool_ref[gi].astype(BF16)))
    p_out = jnp.concatenate(pouts, axis=1) * pscale_ref[...]
    hist_s[...] = ext[tl:, :]

    mixh = jnp.concatenate([jnp.concatenate(r, axis=0) for r in head_rows], axis=1)
    mix = _dot(mixh, wout_s[0:MLSTM_WIDTH, :]) + _dot(p_out.astype(BF16), wout_s[MLSTM_WIDTH:, :])
    x1_ref[...] = x + ga1 * (_rms(mix) * gpost_ref[...])

    @pl.when(t == nt - 1)
    def _():
        for h in range(N_HEADS):
            c_ref[h] = cns[h][:, 0:HEAD_DIM]
            n_ref[h] = cns[h][:, HEAD_DIM:HEAD_DIM + 1]
            m_ref[h] = ms[h]
        pool_ref[...] = ext[tl + POOL_HIST - POOL_BUF:, :]


def _mix_prompt(x, mod_p, g_pre1, g_post1, w_in_t, w_gate_t, gate_bias, g_head, w_pool, pool_scale, w_out):
    bsz, seq, _ = x.shape
    tl = MIX_TOKENS
    nt = seq // tl
    const2 = lambda b, t: (0, 0)
    const3 = lambda b, t: (0, 0, 0)
    once = pl.Buffered(1)
    return pl.pallas_call(
        _mix_kernel,
        grid=(bsz, nt),
        in_specs=[
            pl.BlockSpec((None, tl, D_MODEL), lambda b, t: (b, t, 0)),
            pl.BlockSpec((None, 1, 6 * D_MODEL), lambda b, t: (b, 0, 0)),
            pl.BlockSpec((1, D_MODEL), const2),
            pl.BlockSpec((1, D_MODEL), const2),
            pl.BlockSpec(memory_space=pl.ANY),
            pl.BlockSpec(w_gate_t.shape, const2, pipeline_mode=once),
            pl.BlockSpec((GATE_ROWS, 1), const2),
            pl.BlockSpec((1, HEAD_DIM), const2),
            pl.BlockSpec(w_pool.shape, const3),
            pl.BlockSpec((1, POOL_WIDTH), const2),
            pl.BlockSpec(memory_space=pl.ANY),
        ],
        out_specs=[
            pl.BlockSpec((None, tl, D_MODEL), lambda b, t: (b, t, 0)),
            pl.BlockSpec((None, N_HEADS, HEAD_DIM, HEAD_DIM), lambda b, t: (b, 0, 0, 0)),
            pl.BlockSpec((None, N_HEADS, HEAD_DIM, 1), lambda b, t: (b, 0, 0, 0)),
            pl.BlockSpec((None, N_HEADS, 1, 1), lambda b, t: (b, 0, 0, 0)),
            pl.BlockSpec((None, POOL_BUF, POOL_WIDTH), lambda b, t: (b, 0, 0)),
        ],
        out_shape=[
            jax.ShapeDtypeStruct((bsz, seq, D_MODEL), F32),
            jax.ShapeDtypeStruct((bsz, N_HEADS, HEAD_DIM, HEAD_DIM), F32),
            jax.ShapeDtypeStruct((bsz, N_HEADS, HEAD_DIM, 1), F32),
            jax.ShapeDtypeStruct((bsz, N_HEADS, 1, 1), F32),
            jax.ShapeDtypeStruct((bsz, POOL_BUF, POOL_WIDTH), F32),
        ],
        scratch_shapes=[
            pltpu.VMEM((D_MODEL, 3 * MLSTM_WIDTH + POOL_WIDTH), BF16),
            pltpu.VMEM((MLSTM_WIDTH, D_MODEL), BF16),
            pltpu.VMEM((GATE_ROWS, D_MODEL), BF16),
            pltpu.VMEM((D_MODEL, D_MODEL), BF16),
            pltpu.VMEM((2, MLSTM_WIDTH, D_MODEL), F32),
            pltpu.SemaphoreType.DMA((2,)),
            pltpu.VMEM((N_HEADS, HEAD_DIM, 2 * HEAD_DIM), F32),
            pltpu.VMEM((N_HEADS, 1, 1), F32),
            pltpu.VMEM((POOL_HIST, POOL_WIDTH), F32),
        ],
        compiler_params=pltpu.CompilerParams(
            dimension_semantics=("arbitrary", "arbitrary"), vmem_limit_bytes=VMEM_LIMIT),
        name="mix_prompt",
    )(x, mod_p, g_pre1, g_post1, w_in_t, w_gate_t, gate_bias, g_head, w_pool, pool_scale, w_out)


def _mlp_rows(x, mod, gpre, gpost, wup_s, wdn_s):
    sh2 = mod[:, 3 * D_MODEL:4 * D_MODEL]
    sc2 = mod[:, 4 * D_MODEL:5 * D_MODEL]
    ga2 = mod[:, 5 * D_MODEL:6 * D_MODEL]
    hn = (_rms(x) * gpre) * (1.0 + sc2) + sh2
    hb = hn.astype(BF16)
    acc = jnp.zeros(x.shape, F32)
    for j in range(D_FF // FF_CHUNK):
        f = _dot(hb, wup_s[:, j * FF_CHUNK:(j + 1) * FF_CHUNK])
        f = jnp.square(jnp.maximum(f, 0.0))
        acc = acc + _dot(f.astype(BF16), wdn_s[j * FF_CHUNK:(j + 1) * FF_CHUNK, :])
    return x + ga2 * (_rms(acc) * gpost)


def _mlp_kernel(xp_ref, xs_ref, modp_ref, mods_ref, gpre_ref, gpost_ref, wup_hbm, wdn_hbm,
                yp_ref, ys_ref, wup_s, wdn_s, stage_u, stage_d, sem):
    i = pl.program_id(0)
    n_prompt = pl.num_programs(0) - 1

    @pl.when(i == 0)
    def _():
        ru = stage_u.shape[1]
        rd = stage_d.shape[1]

        def sink_u(k, blk):
            wup_s[k * ru:(k + 1) * ru, :] = blk.astype(BF16)

        def sink_d(k, blk):
            wdn_s[k * rd:(k + 1) * rd, :] = blk.astype(BF16)

        _stream_cast([wup_hbm.at[0, pl.ds(k * ru, ru), :] for k in range(D_MODEL // ru)], stage_u, sem, sink_u)
        _stream_cast([wdn_hbm.at[0, pl.ds(k * rd, rd), :] for k in range(D_FF // rd)], stage_d, sem, sink_d)

    @pl.when(i < n_prompt)
    def _():
        yp_ref[...] = _mlp_rows(xp_ref[...], modp_ref[...], gpre_ref[...], gpost_ref[...], wup_s, wdn_s)

    @pl.when(i == n_prompt)
    def _():
        ys_ref[...] = _mlp_rows(xs_ref[...], mods_ref[...], gpre_ref[...], gpost_ref[...], wup_s, wdn_s)


def _mlp(xp, xs, mod_p, mod_s, rows_per_mod, g_pre2, g_post2, w_up, w_dn):
    n_p = xp.shape[0]
    n_s = xs.shape[0]
    tm = MLP_ROWS
    n_tiles = n_p // tm
    steps_per_mod = rows_per_mod // tm
    last = n_tiles - 1
    const2 = lambda i: (0, 0)
    ptile = lambda i: (jnp.minimum(i, last), 0)
    return pl.pallas_call(
        _mlp_kernel,
        grid=(n_tiles + 1,),
        in_specs=[
            pl.BlockSpec((tm, D_MODEL), ptile),
            pl.BlockSpec((n_s, D_MODEL), const2),
            pl.BlockSpec((None, 1, 6 * D_MODEL), lambda i: (jnp.minimum(i, last) // steps_per_mod, 0, 0)),
            pl.BlockSpec((n_s, 6 * D_MODEL), const2),
            pl.BlockSpec((1, D_MODEL), const2),
            pl.BlockSpec((1, D_MODEL), const2),
            pl.BlockSpec(memory_space=pl.ANY),
            pl.BlockSpec(memory_space=pl.ANY),
        ],
        out_specs=[
            pl.BlockSpec((tm, D_MODEL), ptile),
            pl.BlockSpec((n_s, D_MODEL), const2),
        ],
        out_shape=[
            jax.ShapeDtypeStruct((n_p, D_MODEL), F32),
            jax.ShapeDtypeStruct((n_s, D_MODEL), F32),
        ],
        scratch_shapes=[
            pltpu.VMEM((D_MODEL, D_FF), BF16),
            pltpu.VMEM((D_FF, D_MODEL), BF16),
            pltpu.VMEM((2, 128, D_FF), F32),
            pltpu.VMEM((2, 512, D_MODEL), F32),
            pltpu.SemaphoreType.DMA((2,)),
        ],
        compiler_params=pltpu.CompilerParams(
            dimension_semantics=("arbitrary",), vmem_limit_bytes=VMEM_LIMIT),
        name="mlp",
    )(xp, xs, mod_p, mod_s, g_pre2, g_post2, w_up, w_dn)


def _sproj_kernel(x_ref, mod_ref, gpre_ref, w_ref, z_ref):
    x = x_ref[...]
    mod = mod_ref[...]
    sh1 = mod[:, 0:D_MODEL]
    sc1 = mod[:, D_MODEL:2 * D_MODEL]
    hn = (_rms(x) * gpre_ref[...]) * (1.0 + sc1) + sh1
    z_ref[...] = _dot_nt(hn.astype(BF16), w_ref[...].astype(BF16))


def _sample_proj(x, mod_s, g_pre1, w_in_t):
    n = x.shape[0]
    return pl.pallas_call(
        _sproj_kernel,
        out_shape=jax.ShapeDtypeStruct((n, IN_COLS), F32),
        compiler_params=pltpu.CompilerParams(vmem_limit_bytes=VMEM_LIMIT),
        name="sample_proj",
    )(x, mod_s, g_pre1, w_in_t)


def _sample_gates(gates, m0):
    ig = gates[:, 0:N_HEADS]
    logf = jax.nn.log_sigmoid(gates[:, N_HEADS:2 * N_HEADS])
    g = logf + m0
    m = jnp.maximum(g, ig)
    return m, jnp.exp(ig - m), jnp.exp(g - m)


def _sstate_kernel(z_ref, m0_ref, gb_ref, c0_ref, c1_ref, qc_ref):
    z = z_ref[...]
    gates = z[:, _CG:_CG + 2 * N_HEADS] + gb_ref[...]
    _, w, a = _sample_gates(gates, m0_ref[...])
    sub = lax.broadcasted_iota(jnp.int32, (8, HEAD_DIM), 0)
    for j in range(STATE_TOKENS):
        rows = []
        for h in range(N_HEADS):
            q = z[j:j + 1, _CQ + h * HEAD_DIM:_CQ + (h + 1) * HEAD_DIM]
            k = z[j:j + 1, _CK + h * HEAD_DIM:_CK + (h + 1) * HEAD_DIM] * K_SCALE
            v = z[j:j + 1, _CV + h * HEAD_DIM:_CV + (h + 1) * HEAD_DIM]
            ah = a[j:j + 1, h:h + 1]
            wh = w[j:j + 1, h:h + 1]
            c0 = c0_ref[j, h]
            q8 = jnp.broadcast_to(q, (8, HEAD_DIM)).astype(BF16)
            rows.append(_dot(q8, c0.astype(BF16))[0:1, :])
            kw8 = jnp.where(sub == 0, jnp.broadcast_to(k * wh, (8, HEAD_DIM)), 0.0).astype(BF16)
            v8 = jnp.broadcast_to(v, (8, HEAD_DIM)).astype(BF16)
            c1_ref[j, h] = ah * c0 + _dot_tn(kw8, v8)
        qc_ref[j:j + 1, :] = jnp.concatenate(rows, axis=1)


def _sample_state(z_s, m0, gate_bias_row, c0):
    n = z_s.shape[0]
    bt = STATE_TOKENS
    return pl.pallas_call(
        _sstate_kernel,
        grid=(n // bt,),
        in_specs=[
            pl.BlockSpec((bt, z_s.shape[1]), lambda i: (i, 0)),
            pl.BlockSpec((bt, N_HEADS), lambda i: (i, 0)),
            pl.BlockSpec((1, 2 * N_HEADS), lambda i: (0, 0)),
            pl.BlockSpec((bt, N_HEADS, HEAD_DIM, HEAD_DIM), lambda i: (i, 0, 0, 0)),
        ],
        out_specs=[
            pl.BlockSpec((bt, N_HEADS, HEAD_DIM, HEAD_DIM), lambda i: (i, 0, 0, 0)),
            pl.BlockSpec((bt, MLSTM_WIDTH), lambda i: (i, 0)),
        ],
        out_shape=[
            jax.ShapeDtypeStruct(c0.shape, F32),
            jax.ShapeDtypeStruct((n, MLSTM_WIDTH), F32),
        ],
        compiler_params=pltpu.CompilerParams(
            dimension_semantics=("arbitrary",), vmem_limit_bytes=VMEM_LIMIT),
        name="sample_state",
    )(z_s, m0, gate_bias_row, c0)


def _smix_kernel(x_ref, z_ref, qc_ref, mod_ref, n0_ref, m0_ref, pool_ref, gb_ref, ghead_ref,
                 wpool_ref, pscale_ref, wout_ref, gpost_ref,
                 x1_ref, n1_ref, m1_ref, poolo_ref):
    x = x_ref[...]
    z = z_ref[...]
    mod = mod_ref[...]
    ga1 = mod[:, 2 * D_MODEL:3 * D_MODEL]
    gates = z[:, _CG:_CG + 2 * N_HEADS] + gb_ref[...]
    m, w, a = _sample_gates(gates, m0_ref[...])
    m1_ref[...] = m
    ghead = ghead_ref[...]
    heads = []
    for h in range(N_HEADS):
        sl = slice(h * HEAD_DIM, (h + 1) * HEAD_DIM)
        q = z[:, _CQ + h * HEAD_DIM:_CQ + (h + 1) * HEAD_DIM]
        k = z[:, _CK + h * HEAD_DIM:_CK + (h + 1) * HEAD_DIM] * K_SCALE
        v = z[:, _CV + h * HEAD_DIM:_CV + (h + 1) * HEAD_DIM]
        o = z[:, _CO + h * HEAD_DIM:_CO + (h + 1) * HEAD_DIM]
        n0 = n0_ref[:, sl]
        ah = a[:, h:h + 1]
        wh = w[:, h:h + 1]
        s = jnp.sum(q * k, axis=-1, keepdims=True) * wh
        num = ah * qc_ref[:, sl] + s * v
        den = ah * jnp.sum(q * n0, axis=-1, keepdims=True) + s
        hh = num / jnp.maximum(jnp.abs(den), jnp.exp(-m[:, h:h + 1]))
        n1_ref[:, sl] = ah * n0 + wh * k
        hh = (_rms(hh) * ghead) * jax.nn.sigmoid(o)
        heads.append(hh.astype(BF16))

    u = z[:, _CU:_CU + POOL_WIDTH]
    g = POOL_GROUP_DIM
    pouts = []
    for gi, win in enumerate(POOL_WINDOWS):
        ug = u[:, gi * g:(gi + 1) * g]
        wsum = ug
        for r in range(POOL_BUF - (win - 1), POOL_BUF):
            wsum = wsum + pool_ref[r, :, gi * g:(gi + 1) * g]
        cnt = min(PAST_LEN + 1.0, float(win))
        pooled = wsum / cnt - ug
        pouts.append(_dot(pooled.astype(BF16), wpool_ref[gi].astype(BF16)))
    p_out = jnp.concatenate(pouts, axis=1) * pscale_ref[...]
    poolo_ref[0:POOL_BUF - 1] = pool_ref[1:POOL_BUF]
    poolo_ref[POOL_BUF - 1] = u

    mixh = jnp.concatenate(heads, axis=1)
    mix = (_dot(mixh, wout_ref[0:MLSTM_WIDTH, :].astype(BF16))
           + _dot(p_out.astype(BF16), wout_ref[MLSTM_WIDTH:, :].astype(BF16)))
    x1_ref[...] = x + ga1 * (_rms(mix) * gpost_ref[...])


def _sample_mix(x, z_s, qc, mod_s, n0, m0, pool2d, gate_bias_row, g_head, w_pool, pool_scale, w_out, g_post1):
    n = x.shape[0]
    return pl.pallas_call(
        _smix_kernel,
        out_shape=[
            jax.ShapeDtypeStruct((n, D_MODEL), F32),
            jax.ShapeDtypeStruct((n, MLSTM_WIDTH), F32),
            jax.ShapeDtypeStruct((n, N_HEADS), F32),
            jax.ShapeDtypeStruct(pool2d.shape, F32),
        ],
        compiler_params=pltpu.CompilerParams(vmem_limit_bytes=VMEM_LIMIT),
        name="sample_mix",
    )(x, z_s, qc, mod_s, n0, m0, pool2d, gate_bias_row, g_head, w_pool, pool_scale, w_out, g_post1)


def kernel(x_prompt, x_sample, c_prompt, c_sample, state_C, state_n, state_m, state_pool, w_ada, b_ada,
           g_pre1, g_post1, w_in, b_ig, b_fg, g_head, w_pool, pool_scale, w_out, g_pre2, g_post2,
           w_up, w_down):
    depth = w_ada.shape[0]
    assert depth == 1, "single-layer step"
    bsz, seq, _ = x_prompt.shape
    nb = x_sample.shape[0]
    assert x_sample.shape[1] == 1 and seq % MIX_TOKENS == 0 and nb % STATE_TOKENS == 0
    assert w_in.shape[2] == IN_COLS
    l = 0

    gate_bias = jnp.concatenate([b_ig[l], b_fg[l]])
    gate_bias_col = jnp.pad(gate_bias, (0, GATE_ROWS - 2 * N_HEADS))[:, None]
    gate_bias_row = gate_bias[None, :]
    gpre1 = g_pre1[l][None, :]
    gpost1 = g_post1[l][None, :]
    gpre2 = g_pre2[l][None, :]
    gpost2 = g_post2[l][None, :]
    ghead = g_head[l][None, :]
    pscale = pool_scale[l][None, :]
    w_in_t = jnp.transpose(w_in[l])
    w_gate_t = w_in_t[_CG:_CG + 2 * N_HEADS]

    mod_p, mod_s = _ada(c_prompt, c_sample, w_ada[l], b_ada[l][None, :])
    mod_p = mod_p.reshape(bsz, 1, 6 * D_MODEL)

    x1p, c_p, n_p, m_p, pool_p = _mix_prompt(
        x_prompt, mod_p, gpre1, gpost1, w_in_t, w_gate_t, gate_bias_col, ghead, w_pool[l], pscale, w_out)

    xs = x_sample.reshape(nb, D_MODEL)
    m0 = state_m[l]
    z_s = _sample_proj(xs, mod_s, gpre1, w_in_t)
    c_s, qc = _sample_state(z_s, m0, gate_bias_row, state_C[l])
    x1s, n_s, m_s, pool_s = _sample_mix(
        xs, z_s, qc, mod_s, state_n[l].reshape(nb, MLSTM_WIDTH), m0,
        jnp.transpose(state_pool[l], (1, 0, 2)), gate_bias_row, ghead, w_pool[l], pscale,
        w_out[l], gpost1)

    y_p, y_s = _mlp(x1p.reshape(bsz * seq, D_MODEL), x1s, mod_p, mod_s, seq, gpre2, gpost2, w_up, w_down)

    return (y_p.reshape(bsz, seq, D_MODEL), y_s.reshape(nb, 1, D_MODEL),
            c_p[None], n_p.reshape(1, bsz, N_HEADS, HEAD_DIM), m_p.reshape(1, bsz, N_HEADS), pool_p[None],
            c_s[None], n_s.reshape(1, nb, N_HEADS, HEAD_DIM), m_s[None],
            jnp.transpose(pool_s, (1, 0, 2))[None])
```

```python
import functools

import jax
import jax.numpy as jnp
from jax import lax
from jax.experimental import pallas as pl
from jax.experimental.pallas import tpu as pltpu

F32 = jnp.float32
BF16 = jnp.bfloat16

D_MODEL = 1024
N_HEADS = 4
HEAD_DIM = 128
MLSTM_WIDTH = N_HEADS * HEAD_DIM
POOL_WIDTH = 512
POOL_WINDOWS = (2, 4, 8, 16)
POOL_GROUP_DIM = 128
POOL_BUF = 15
POOL_HIST = 16
D_FF = 4 * D_MODEL
EPS = 1e-6
PAST_LEN = 16384
K_SCALE = HEAD_DIM ** -0.5
GATE_ROWS = 16

_CQ, _CK, _CV, _CO = 0, 512, 1024, 1536
_CG = 2048
_CU = _CG + 2 * N_HEADS
IN_COLS = _CU + POOL_WIDTH

TILE = 512
CHUNK = 256
FF_CHUNK = 1024
STATE_TOKENS = 8
VMEM_LIMIT = 60 * 1024 * 1024


def _dot(a, b):
    return jnp.dot(a, b, preferred_element_type=F32)


def _dot_nt(a, b):
    return lax.dot_general(a, b, (((1,), (1,)), ((), ())), preferred_element_type=F32)


def _dot_tn(a, b):
    return lax.dot_general(a, b, (((0,), (0,)), ((), ())), preferred_element_type=F32)


def _rms(x):
    return x * lax.rsqrt(jnp.mean(x * x, axis=-1, keepdims=True) + EPS)


def _stream_cast(srcs, stage, sem, sink):
    def copy(i):
        return pltpu.make_async_copy(srcs[i], stage.at[i % 2], sem.at[i % 2])

    copy(0).start()
    for i in range(len(srcs)):
        if i + 1 < len(srcs):
            copy(i + 1).start()
        copy(i).wait()
        sink(i, stage[i % 2])


def _ada_kernel(cp_ref, cs_ref, w_ref, b_ref, op_ref, os_ref):
    c = jnp.concatenate([cp_ref[...], cs_ref[...]], axis=0)
    s = c * jax.nn.sigmoid(c)
    mod = _dot(s.astype(BF16), w_ref[...].astype(BF16)) + b_ref[...]
    n_p = cp_ref.shape[0]
    op_ref[...] = mod[0:n_p]
    os_ref[...] = mod[n_p:]


def _ada(c_p, c_s, w_ada, b_ada):
    n_p = c_p.shape[0]
    n_s = c_s.shape[0]
    tn = 1024
    return pl.pallas_call(
        _ada_kernel,
        grid=(6 * D_MODEL // tn,),
        in_specs=[
            pl.BlockSpec((n_p, D_MODEL), lambda j: (0, 0)),
            pl.BlockSpec((n_s, D_MODEL), lambda j: (0, 0)),
            pl.BlockSpec((D_MODEL, tn), lambda j: (0, j)),
            pl.BlockSpec((1, tn), lambda j: (0, j)),
        ],
        out_specs=[
            pl.BlockSpec((n_p, tn), lambda j: (0, j)),
            pl.BlockSpec((n_s, tn), lambda j: (0, j)),
        ],
        out_shape=[
            jax.ShapeDtypeStruct((n_p, 6 * D_MODEL), F32),
            jax.ShapeDtypeStruct((n_s, 6 * D_MODEL), F32),
        ],
        compiler_params=pltpu.CompilerParams(
            dimension_semantics=("arbitrary",), vmem_limit_bytes=VMEM_LIMIT),
        name="ada_mod",
    )(c_p, c_s, w_ada, b_ada)


def _cumsum_lanes(x, upper):
    hi = x.astype(BF16)
    r1 = x - hi.astype(F32)
    mid = r1.astype(BF16)
    lo = (r1 - mid.astype(F32)).astype(BF16)
    y = _dot(jnp.concatenate([hi, mid, lo], axis=0), upper)
    n = x.shape[0]
    return y[0:n] + y[n:2 * n] + y[2 * n:3 * n]


def _layer_kernel(x_ref, xs1_ref, hs2_ref, ga2s_ref, modp_ref, gpre1_ref, gpost1_ref, gpre2_ref, gpost2_ref,
                  winT_hbm, wgate_ref, gb_ref, ghead_ref, wpool_ref, pscale_ref, wout_hbm, wup_hbm, wdn_hbm,
                  y_ref, ys_ref, c_ref, n_ref, m_ref, pool_ref,
                  wtm_s, wkt_s, wgt_s, wout_s, wup_s, wdn_s, sem, cn_s, m_s, hist_s, x1_s, hn2_s,
                  *, tiles_per_seq):
    tl = TILE
    lc = CHUNK
    s = pl.program_id(0)
    n_tiles = pl.num_programs(0) - 1
    n_s = xs1_ref.shape[0]

    @pl.when(s == 0)
    def _():
        half = D_MODEL // 2
        srcs = [winT_hbm.at[pl.ds(c, MLSTM_WIDTH), :] for c in (_CQ, _CK, _CV, _CO, _CU)]
        srcs += [wout_hbm.at[0, pl.ds(r, half), :] for r in (0, half)]
        n_mix = len(srcs)
        srcs += [wup_hbm.at[0, pl.ds(r, half), pl.ds(c, D_MODEL)]
                 for c in range(0, D_FF, D_MODEL) for r in (0, half)]
        n_up = len(srcs)
        srcs += [wdn_hbm.at[0, pl.ds(r, half), :] for r in range(0, D_FF, half)]
        tm_col = {0: 0, 2: MLSTM_WIDTH, 3: 2 * MLSTM_WIDTH, 4: 3 * MLSTM_WIDTH}

        def sink(i, blk):
            if i == 1:
                wkt_s[...] = blk.astype(BF16)
            elif i in tm_col:
                wtm_s[:, tm_col[i]:tm_col[i] + MLSTM_WIDTH] = blk.T.astype(BF16)
            elif i < n_mix:
                wout_s[(i - 5) * half:(i - 4) * half, :] = blk.astype(BF16)
            elif i < n_up:
                k = i - n_mix
                wup_s[(k % 2) * half:(k % 2 + 1) * half, (k // 2) * D_MODEL:(k // 2 + 1) * D_MODEL] = blk.astype(BF16)
            else:
                k = i - n_up
                wdn_s[k * half:(k + 1) * half, :] = blk.astype(BF16)

        _stream_cast(srcs, x1_s, sem, sink)
        pad = jnp.zeros((GATE_ROWS - 2 * N_HEADS, D_MODEL), F32)
        wgt_s[...] = jnp.concatenate([wgate_ref[...], pad], axis=0).astype(BF16)
        cn_s[...] = jnp.zeros_like(cn_s)
        m_s[...] = jnp.zeros_like(m_s)
        hist_s[...] = jnp.zeros_like(hist_s)
        x1_s[...] = jnp.zeros_like(x1_s)
        hn2_s[...] = jnp.zeros_like(hn2_s)
        x1_s[1, 0:n_s, :] = xs1_ref[...]
        hn2_s[1, 0:n_s, :] = hs2_ref[...]

    tile = jnp.minimum(s, n_tiles - 1)
    b = tile // tiles_per_seq
    t = tile % tiles_per_seq
    b_prev = jnp.maximum(s - 1, 0) // tiles_per_seq
    wslot = s % 2
    rslot = 1 - wslot

    hb_prev = hn2_s[rslot]
    x1_prev = x1_s[rslot]
    mlp_acc = [jnp.zeros((tl, D_MODEL), F32)]

    def mlp_slice(j):
        f = _dot(hb_prev, wup_s[:, j * FF_CHUNK:(j + 1) * FF_CHUNK])
        f = jnp.square(jnp.maximum(f, 0.0))
        mlp_acc[0] = mlp_acc[0] + _dot(f.astype(BF16), wdn_s[j * FF_CHUNK:(j + 1) * FF_CHUNK, :])

    x = x_ref[...]
    mod = modp_ref[pl.ds(b, 1), :]
    sh1 = mod[:, 0:D_MODEL]
    sc1 = mod[:, D_MODEL:2 * D_MODEL]
    ga1 = mod[:, 2 * D_MODEL:3 * D_MODEL]
    sh2 = mod[:, 3 * D_MODEL:4 * D_MODEL]
    sc2 = mod[:, 4 * D_MODEL:5 * D_MODEL]
    hn = (_rms(x) * gpre1_ref[...]) * (1.0 + sc1) + sh1
    hnb = hn.astype(BF16)
    gt = _dot_nt(wgt_s[...], hnb) + gb_ref[...]
    logf_all = jax.nn.log_sigmoid(gt)
    row_i = lax.broadcasted_iota(jnp.int32, (lc, lc), 0)
    col_i = lax.broadcasted_iota(jnp.int32, (lc, lc), 1)
    causal = col_i <= row_i
    upper = (row_i <= col_i).astype(BF16)
    ones_blk = jnp.ones((lc, HEAD_DIM), BF16)
    ghead = ghead_ref[...]
    n_chunks = tl // lc
    lane_tiles = lc // HEAD_DIM
    bcs_rows = []
    bcs_cols = []
    for c in range(n_chunks):
        b16 = _cumsum_lanes(logf_all[:, c * lc:(c + 1) * lc], upper)
        bcs_rows.append(b16[N_HEADS:2 * N_HEADS])
        bcs_cols.append(b16.T)

    mlp_slice(0)
    ztm = _dot(hnb, wtm_s[...])
    kt_all = _dot_nt(wkt_s[...], hnb)

    fresh = t == 0
    cns = [jnp.where(fresh, 0.0, cn_s[h]) for h in range(N_HEADS)]
    ms = [jnp.where(fresh, 0.0, m_s[h]) for h in range(N_HEADS)]
    head_rows = [[] for _ in range(N_HEADS)]
    for c in range(n_chunks):
        tok = slice(c * lc, (c + 1) * lc)
        bcs = bcs_rows[c]
        rr = gt[0:N_HEADS, tok] - bcs
        for h in range(N_HEADS):
            sl = slice(h * HEAD_DIM, (h + 1) * HEAD_DIM)
            q = ztm[tok, sl]
            v = ztm[tok, MLSTM_WIDTH + h * HEAD_DIM:MLSTM_WIDTH + (h + 1) * HEAD_DIM]
            o = ztm[tok, 2 * MLSTM_WIDTH + h * HEAD_DIM:2 * MLSTM_WIDTH + (h + 1) * HEAD_DIM]
            kt = kt_all[sl, tok] * K_SCALE
            r_row = rr[h:h + 1, :]
            m0 = ms[h]
            cn = cns[h]

            rmat = jnp.where(causal, r_row, -jnp.inf)
            mcol = jnp.maximum(jnp.max(rmat, axis=-1, keepdims=True), m0)
            mcol_b = jnp.broadcast_to(mcol, (lc, HEAD_DIM))
            wmat = jnp.exp(rmat - jnp.concatenate([mcol_b] * lane_tiles, axis=1))
            a_b = jnp.exp(m0 - mcol_b)
            bcol = bcs_cols[c][:, N_HEADS + h:N_HEADS + h + 1]

            qb = q.astype(BF16)
            vext = jnp.concatenate([v.astype(BF16), ones_blk], axis=1)
            sc = _dot(qb, kt.astype(BF16)) * wmat
            qcn = _dot(qb, cn.astype(BF16))
            sv = _dot(sc.astype(BF16), vext)
            nd = jnp.concatenate([a_b, a_b], axis=1) * qcn + sv
            num = nd[:, 0:HEAD_DIM]
            den = nd[:, HEAD_DIM:]
            hh = num / jnp.maximum(jnp.abs(den), jnp.exp(-(bcol + mcol)))
            hh = (_rms(hh) * ghead) * jax.nn.sigmoid(o)
            head_rows[h].append(hh.astype(BF16))

            ml = mcol[lc - 1:lc, :]
            al = jnp.exp(m0 - ml)
            wl = jnp.exp(r_row - ml)
            kw = (kt * wl).astype(BF16)
            cns[h] = al * cn + _dot(kw, vext)
            ms[h] = bcs[h:h + 1, lc - 1:lc] + ml
        mlp_slice(1 + c)
    for h in range(N_HEADS):
        cn_s[h] = cns[h]
        m_s[h] = ms[h]

    u = ztm[:, 3 * MLSTM_WIDTH:3 * MLSTM_WIDTH + POOL_WIDTH]
    hist = jnp.where(fresh, 0.0, hist_s[...])
    ext = jnp.concatenate([hist, u], axis=0)
    g = POOL_GROUP_DIM
    p2 = ext + pltpu.roll(ext, 1, axis=0)
    p4 = p2[:, g:] + pltpu.roll(p2[:, g:], 2, axis=0)
    p8 = p4[:, g:] + pltpu.roll(p4[:, g:], 4, axis=0)
    p16 = p8[:, g:] + pltpu.roll(p8[:, g:], 8, axis=0)
    wsum = (p2[:, 0:g], p4[:, 0:g], p8[:, 0:g], p16)
    pos = (t * tl + lax.broadcasted_iota(jnp.int32, (tl, 1), 0)).astype(F32)
    pouts = []
    for gi, win in enumerate(POOL_WINDOWS):
        cnt = jnp.minimum(pos + 1.0, float(win))
        ug = u[:, gi * g:(gi + 1) * g]
        pooled = wsum[gi][POOL_HIST:, :] / cnt - ug
        pouts.append(_dot(pooled.astype(BF16), wpool_ref[gi].astype(BF16)))
    p_out = jnp.concatenate(pouts, axis=1) * pscale_ref[...]
    hist_s[...] = ext[tl:, :]

    mixh = jnp.concatenate([jnp.concatenate(r, axis=0) for r in head_rows], axis=1)
    mix = _dot(mixh, wout_s[0:MLSTM_WIDTH, :]) + _dot(p_out.astype(BF16), wout_s[MLSTM_WIDTH:, :])

    mlp_slice(D_FF // FF_CHUNK - 1)
    ga2_prev = modp_ref[pl.ds(b_prev, 1), 5 * D_MODEL:6 * D_MODEL]
    mlp_out = _rms(mlp_acc[0]) * gpost2_ref[...]
    y_ref[...] = x1_prev + ga2_prev * mlp_out

    x1 = x + ga1 * (_rms(mix) * gpost1_ref[...])
    x1_s[wslot] = x1
    hn2_s[wslot] = ((_rms(x1) * gpre2_ref[...]) * (1.0 + sc2) + sh2).astype(BF16)

    @pl.when(s == 0)
    def _():
        ys_ref[...] = x1_prev[0:n_s] + ga2s_ref[...] * mlp_out[0:n_s]

    @pl.when((t == tiles_per_seq - 1) & (s < n_tiles))
    def _():
        for h in range(N_HEADS):
            c_ref[h] = cns[h][:, 0:HEAD_DIM]
            n_ref[h] = cns[h][:, HEAD_DIM:HEAD_DIM + 1]
            m_ref[h] = ms[h]
        pool_ref[...] = ext[tl + POOL_HIST - POOL_BUF:, :]


def _prompt_layer(x, xs1, hs2, mod_p, mod_s, g_pre1, g_post1, g_pre2, g_post2, w_in_t, w_gate_t, gate_bias,
                  g_head, w_pool, pool_scale, w_out, w_up, w_dn):
    bsz, seq, _ = x.shape
    n_s = xs1.shape[0]
    tl = TILE
    tiles_per_seq = seq // tl
    n_tiles = bsz * tiles_per_seq
    last = n_tiles - 1
    const2 = lambda s: (0, 0)
    const3 = lambda s: (0, 0, 0)
    once = pl.Buffered(1)
    mix_tile = lambda s: jnp.minimum(s, last)
    state_blk = lambda s: (mix_tile(s) // tiles_per_seq, 0, 0, 0)
    return pl.pallas_call(
        functools.partial(_layer_kernel, tiles_per_seq=tiles_per_seq),
        grid=(n_tiles + 1,),
        in_specs=[
            pl.BlockSpec((tl, D_MODEL), lambda s: (mix_tile(s), 0)),
            pl.BlockSpec((n_s, D_MODEL), const2, pipeline_mode=once),
            pl.BlockSpec((n_s, D_MODEL), const2, pipeline_mode=once),
            pl.BlockSpec((n_s, D_MODEL), lambda s: (0, 5), pipeline_mode=once),
            pl.BlockSpec((bsz, 6 * D_MODEL), const2, pipeline_mode=once),
            pl.BlockSpec((1, D_MODEL), const2),
            pl.BlockSpec((1, D_MODEL), const2),
            pl.BlockSpec((1, D_MODEL), const2),
            pl.BlockSpec((1, D_MODEL), const2),
            pl.BlockSpec(memory_space=pl.ANY),
            pl.BlockSpec(w_gate_t.shape, const2, pipeline_mode=once),
            pl.BlockSpec((GATE_ROWS, 1), const2),
            pl.BlockSpec((1, HEAD_DIM), const2),
            pl.BlockSpec(w_pool.shape, const3, pipeline_mode=once),
            pl.BlockSpec((1, POOL_WIDTH), const2),
            pl.BlockSpec(memory_space=pl.ANY),
            pl.BlockSpec(memory_space=pl.ANY),
            pl.BlockSpec(memory_space=pl.ANY),
        ],
        out_specs=[
            pl.BlockSpec((tl, D_MODEL), lambda s: (jnp.maximum(s - 1, 0), 0)),
            pl.BlockSpec((n_s, D_MODEL), const2),
            pl.BlockSpec((None, N_HEADS, HEAD_DIM, HEAD_DIM), state_blk),
            pl.BlockSpec((None, N_HEADS, HEAD_DIM, 1), state_blk),
            pl.BlockSpec((None, N_HEADS, 1, 1), state_blk),
            pl.BlockSpec((None, POOL_BUF, POOL_WIDTH), lambda s: (mix_tile(s) // tiles_per_seq, 0, 0)),
        ],
        out_shape=[
            jax.ShapeDtypeStruct((bsz * seq, D_MODEL), F32),
            jax.ShapeDtypeStruct((n_s, D_MODEL), F32),
            jax.ShapeDtypeStruct((bsz, N_HEADS, HEAD_DIM, HEAD_DIM), F32),
            jax.ShapeDtypeStruct((bsz, N_HEADS, HEAD_DIM, 1), F32),
            jax.ShapeDtypeStruct((bsz, N_HEADS, 1, 1), F32),
            jax.ShapeDtypeStruct((bsz, POOL_BUF, POOL_WIDTH), F32),
        ],
        scratch_shapes=[
            pltpu.VMEM((D_MODEL, 3 * MLSTM_WIDTH + POOL_WIDTH), BF16),
            pltpu.VMEM((MLSTM_WIDTH, D_MODEL), BF16),
            pltpu.VMEM((GATE_ROWS, D_MODEL), BF16),
            pltpu.VMEM((D_MODEL, D_MODEL), BF16),
            pltpu.VMEM((D_MODEL, D_FF), BF16),
            pltpu.VMEM((D_FF, D_MODEL), BF16),
            pltpu.SemaphoreType.DMA((2,)),
            pltpu.VMEM((N_HEADS, HEAD_DIM, 2 * HEAD_DIM), F32),
            pltpu.VMEM((N_HEADS, 1, 1), F32),
            pltpu.VMEM((POOL_HIST, POOL_WIDTH), F32),
            pltpu.VMEM((2, tl, D_MODEL), F32),
            pltpu.VMEM((2, tl, D_MODEL), BF16),
        ],
        compiler_params=pltpu.CompilerParams(
            dimension_semantics=("arbitrary",), vmem_limit_bytes=VMEM_LIMIT),
        name="prompt_layer",
    )(x.reshape(bsz * seq, D_MODEL), xs1, hs2, mod_s, mod_p, g_pre1, g_post1, g_pre2, g_post2, w_in_t, w_gate_t,
      gate_bias, g_head, w_pool, pool_scale, w_out, w_up, w_dn)


def _sproj_kernel(x_ref, mod_ref, gpre_ref, w_ref, z_ref):
    x = x_ref[...]
    mod = mod_ref[...]
    sh1 = mod[:, 0:D_MODEL]
    sc1 = mod[:, D_MODEL:2 * D_MODEL]
    hn = (_rms(x) * gpre_ref[...]) * (1.0 + sc1) + sh1
    z_ref[...] = _dot_nt(hn.astype(BF16), w_ref[...].astype(BF16))


def _sample_proj(x, mod_s, g_pre1, w_in_t):
    n = x.shape[0]
    return pl.pallas_call(
        _sproj_kernel,
        out_shape=jax.ShapeDtypeStruct((n, IN_COLS), F32),
        compiler_params=pltpu.CompilerParams(vmem_limit_bytes=VMEM_LIMIT),
        name="sample_proj",
    )(x, mod_s, g_pre1, w_in_t)


def _sample_gates(gates, m0):
    ig = gates[:, 0:N_HEADS]
    logf = jax.nn.log_sigmoid(gates[:, N_HEADS:2 * N_HEADS])
    g = logf + m0
    m = jnp.maximum(g, ig)
    return m, jnp.exp(ig - m), jnp.exp(g - m)


def _sstate_kernel(z_ref, m0_ref, gb_ref, c0_ref, c1_ref, qc_ref):
    z = z_ref[...]
    gates = z[:, _CG:_CG + 2 * N_HEADS] + gb_ref[...]
    _, w, a = _sample_gates(gates, m0_ref[...])
    sub = lax.broadcasted_iota(jnp.int32, (8, HEAD_DIM), 0)
    for j in range(STATE_TOKENS):
        rows = []
        for h in range(N_HEADS):
            q = z[j:j + 1, _CQ + h * HEAD_DIM:_CQ + (h + 1) * HEAD_DIM]
            k = z[j:j + 1, _CK + h * HEAD_DIM:_CK + (h + 1) * HEAD_DIM] * K_SCALE
            v = z[j:j + 1, _CV + h * HEAD_DIM:_CV + (h + 1) * HEAD_DIM]
            ah = a[j:j + 1, h:h + 1]
            wh = w[j:j + 1, h:h + 1]
            c0 = c0_ref[j, h]
            q8 = jnp.broadcast_to(q, (8, HEAD_DIM)).astype(BF16)
            rows.append(_dot(q8, c0.astype(BF16))[0:1, :])
            kw8 = jnp.where(sub == 0, jnp.broadcast_to(k * wh, (8, HEAD_DIM)), 0.0).astype(BF16)
            v8 = jnp.broadcast_to(v, (8, HEAD_DIM)).astype(BF16)
            c1_ref[j, h] = ah * c0 + _dot_tn(kw8, v8)
        qc_ref[j:j + 1, :] = jnp.concatenate(rows, axis=1)


def _sample_state(z_s, m0, gate_bias_row, c0):
    n = z_s.shape[0]
    bt = STATE_TOKENS
    return pl.pallas_call(
        _sstate_kernel,
        grid=(n // bt,),
        in_specs=[
            pl.BlockSpec((bt, z_s.shape[1]), lambda i: (i, 0)),
            pl.BlockSpec((bt, N_HEADS), lambda i: (i, 0)),
            pl.BlockSpec((1, 2 * N_HEADS), lambda i: (0, 0)),
            pl.BlockSpec((bt, N_HEADS, HEAD_DIM, HEAD_DIM), lambda i: (i, 0, 0, 0)),
        ],
        out_specs=[
            pl.BlockSpec((bt, N_HEADS, HEAD_DIM, HEAD_DIM), lambda i: (i, 0, 0, 0)),
            pl.BlockSpec((bt, MLSTM_WIDTH), lambda i: (i, 0)),
        ],
        out_shape=[
            jax.ShapeDtypeStruct(c0.shape, F32),
            jax.ShapeDtypeStruct((n, MLSTM_WIDTH), F32),
        ],
        compiler_params=pltpu.CompilerParams(
            dimension_semantics=("arbitrary",), vmem_limit_bytes=VMEM_LIMIT),
        name="sample_state",
    )(z_s, m0, gate_bias_row, c0)


def _smix_kernel(x_ref, z_ref, qc_ref, mod_ref, n0_ref, m0_ref, pool_ref, gb_ref, ghead_ref,
                 wpool_ref, pscale_ref, wout_ref, gpost_ref, gpre2_ref,
                 x1_ref, h2_ref, n1_ref, m1_ref, poolo_ref):
    x = x_ref[...]
    z = z_ref[...]
    mod = mod_ref[...]
    ga1 = mod[:, 2 * D_MODEL:3 * D_MODEL]
    sh2 = mod[:, 3 * D_MODEL:4 * D_MODEL]
    sc2 = mod[:, 4 * D_MODEL:5 * D_MODEL]
    gates = z[:, _CG:_CG + 2 * N_HEADS] + gb_ref[...]
    m, w, a = _sample_gates(gates, m0_ref[...])
    m1_ref[...] = m
    ghead = ghead_ref[...]
    heads = []
    for h in range(N_HEADS):
        sl = slice(h * HEAD_DIM, (h + 1) * HEAD_DIM)
        q = z[:, _CQ + h * HEAD_DIM:_CQ + (h + 1) * HEAD_DIM]
        k = z[:, _CK + h * HEAD_DIM:_CK + (h + 1) * HEAD_DIM] * K_SCALE
        v = z[:, _CV + h * HEAD_DIM:_CV + (h + 1) * HEAD_DIM]
        o = z[:, _CO + h * HEAD_DIM:_CO + (h + 1) * HEAD_DIM]
        n0 = n0_ref[:, sl]
        ah = a[:, h:h + 1]
        wh = w[:, h:h + 1]
        s = jnp.sum(q * k, axis=-1, keepdims=True) * wh
        num = ah * qc_ref[:, sl] + s * v
        den = ah * jnp.sum(q * n0, axis=-1, keepdims=True) + s
        hh = num / jnp.maximum(jnp.abs(den), jnp.exp(-m[:, h:h + 1]))
        n1_ref[:, sl] = ah * n0 + wh * k
        hh = (_rms(hh) * ghead) * jax.nn.sigmoid(o)
        heads.append(hh.astype(BF16))

    u = z[:, _CU:_CU + POOL_WIDTH]
    g = POOL_GROUP_DIM
    pouts = []
    for gi, win in enumerate(POOL_WINDOWS):
        ug = u[:, gi * g:(gi + 1) * g]
        wsum = ug
        for r in range(POOL_BUF - (win - 1), POOL_BUF):
            wsum = wsum + pool_ref[r, :, gi * g:(gi + 1) * g]
        cnt = min(PAST_LEN + 1.0, float(win))
        pooled = wsum / cnt - ug
        pouts.append(_dot(pooled.astype(BF16), wpool_ref[gi].astype(BF16)))
    p_out = jnp.concatenate(pouts, axis=1) * pscale_ref[...]
    poolo_ref[0:POOL_BUF - 1] = pool_ref[1:POOL_BUF]
    poolo_ref[POOL_BUF - 1] = u

    mixh = jnp.concatenate(heads, axis=1)
    mix = (_dot(mixh, wout_ref[0:MLSTM_WIDTH, :].astype(BF16))
           + _dot(p_out.astype(BF16), wout_ref[MLSTM_WIDTH:, :].astype(BF16)))
    x1 = x + ga1 * (_rms(mix) * gpost_ref[...])
    x1_ref[...] = x1
    h2_ref[...] = ((_rms(x1) * gpre2_ref[...]) * (1.0 + sc2) + sh2).astype(BF16)


def _sample_mix(x, z_s, qc, mod_s, n0, m0, pool_rows, gate_bias_row, g_head, w_pool, pool_scale, w_out, g_post1,
                g_pre2):
    n = x.shape[0]
    return pl.pallas_call(
        _smix_kernel,
        out_shape=[
            jax.ShapeDtypeStruct((n, D_MODEL), F32),
            jax.ShapeDtypeStruct((n, D_MODEL), BF16),
            jax.ShapeDtypeStruct((n, MLSTM_WIDTH), F32),
            jax.ShapeDtypeStruct((n, N_HEADS), F32),
            jax.ShapeDtypeStruct(pool_rows.shape, F32),
        ],
        compiler_params=pltpu.CompilerParams(vmem_limit_bytes=VMEM_LIMIT),
        name="sample_mix",
    )(x, z_s, qc, mod_s, n0, m0, pool_rows, gate_bias_row, g_head, w_pool, pool_scale, w_out, g_post1, g_pre2)


def kernel(x_prompt, x_sample, c_prompt, c_sample, state_C, state_n, state_m, state_pool, w_ada, b_ada,
           g_pre1, g_post1, w_in, b_ig, b_fg, g_head, w_pool, pool_scale, w_out, g_pre2, g_post2,
           w_up, w_down):
    depth = w_ada.shape[0]
    assert depth == 1, "single-layer step"
    bsz, seq, _ = x_prompt.shape
    nb = x_sample.shape[0]
    assert x_sample.shape[1] == 1 and seq % TILE == 0 and nb % STATE_TOKENS == 0 and nb <= TILE
    assert w_in.shape[2] == IN_COLS
    l = 0

    gate_bias = jnp.concatenate([b_ig[l], b_fg[l]])
    gate_bias_col = jnp.pad(gate_bias, (0, GATE_ROWS - 2 * N_HEADS))[:, None]
    gate_bias_row = gate_bias[None, :]
    gpre1 = g_pre1[l][None, :]
    gpost1 = g_post1[l][None, :]
    gpre2 = g_pre2[l][None, :]
    gpost2 = g_post2[l][None, :]
    ghead = g_head[l][None, :]
    pscale = pool_scale[l][None, :]
    w_in_t = jnp.transpose(w_in[l])
    w_gate_t = w_in_t[_CG:_CG + 2 * N_HEADS]

    mod_p, mod_s = _ada(c_prompt, c_sample, w_ada[l], b_ada[l][None, :])

    xs = x_sample.reshape(nb, D_MODEL)
    m0 = state_m[l]
    z_s = _sample_proj(xs, mod_s, gpre1, w_in_t)
    c_s, qc = _sample_state(z_s, m0, gate_bias_row, state_C[l])
    x1s, h2s, n_s, m_s, pool_s = _sample_mix(
        xs, z_s, qc, mod_s, state_n[l].reshape(nb, MLSTM_WIDTH), m0,
        jnp.transpose(state_pool[l], (1, 0, 2)), gate_bias_row, ghead, w_pool[l], pscale,
        w_out[l], gpost1, gpre2)

    y_p, y_s, c_p, n_p, m_p, pool_p = _prompt_layer(
        x_prompt, x1s, h2s, mod_p, mod_s, gpre1, gpost1, gpre2, gpost2, w_in_t, w_gate_t, gate_bias_col,
        ghead, w_pool[l], pscale, w_out, w_up, w_down)

    return (y_p.reshape(bsz, seq, D_MODEL), y_s.reshape(nb, 1, D_MODEL),
            c_p[None], n_p.reshape(1, bsz, N_HEADS, HEAD_DIM), m_p.reshape(1, bsz, N_HEADS), pool_p[None],
            c_s[None], n_s.reshape(1, nb, N_HEADS, HEAD_DIM), m_s[None],
            jnp.transpose(pool_s, (1, 0, 2))[None])
```

```python
import jax
import jax.numpy as jnp
from jax import lax
from jax.experimental import pallas as pl
from jax.experimental.pallas import tpu as pltpu

F32 = jnp.float32
BF16 = jnp.bfloat16

D_MODEL = 1024
N_HEADS = 4
HEAD_DIM = 128
MLSTM_WIDTH = N_HEADS * HEAD_DIM
POOL_WIDTH = 512
POOL_WINDOWS = (2, 4, 8, 16)
POOL_GROUP_DIM = 128
POOL_BUF = 15
POOL_HIST = 16
D_FF = 4 * D_MODEL
EPS = 1e-6
PAST_LEN = 16384
K_SCALE = HEAD_DIM ** -0.5
GATE_ROWS = 16

_CQ, _CK, _CV, _CO = 0, 512, 1024, 1536
_CG = 2048
_CU = _CG + 2 * N_HEADS
IN_COLS = _CU + POOL_WIDTH

MIX_TOKENS = 1024
CHUNK = 256
MLP_ROWS = 1024
MLP_SUB = 512
FF_CHUNK = 1024
STATE_TOKENS = 16
ADA_COLS = 2048
VMEM_LIMIT = 60 * 1024 * 1024


def _dot(a, b):
    return jnp.dot(a, b, preferred_element_type=F32)


def _dot_nt(a, b):
    return lax.dot_general(a, b, (((1,), (1,)), ((), ())), preferred_element_type=F32)


def _dot_tn(a, b):
    return lax.dot_general(a, b, (((0,), (0,)), ((), ())), preferred_element_type=F32)


def _rms(x):
    return x * lax.rsqrt(jnp.mean(x * x, axis=-1, keepdims=True) + EPS)


def _stream_cast(srcs, stage, sem, sink):
    def copy(i):
        return pltpu.make_async_copy(srcs[i], stage.at[i % 2], sem.at[i % 2])

    copy(0).start()
    for i in range(len(srcs)):
        if i + 1 < len(srcs):
            copy(i + 1).start()
        copy(i).wait()
        sink(i, stage[i % 2])


def _ada_kernel(cp_ref, cs_ref, w_ref, b_ref, op_ref, os_ref):
    c = jnp.concatenate([cp_ref[...], cs_ref[...]], axis=0)
    s = c * jax.nn.sigmoid(c)
    mod = _dot(s.astype(BF16), w_ref[...].astype(BF16)) + b_ref[...]
    n_p = cp_ref.shape[0]
    op_ref[...] = mod[0:n_p]
    os_ref[...] = mod[n_p:]


def _ada(c_p, c_s, w_ada, b_ada):
    n_p = c_p.shape[0]
    n_s = c_s.shape[0]
    tn = ADA_COLS
    return pl.pallas_call(
        _ada_kernel,
        grid=(6 * D_MODEL // tn,),
        in_specs=[
            pl.BlockSpec((n_p, D_MODEL), lambda j: (0, 0)),
            pl.BlockSpec((n_s, D_MODEL), lambda j: (0, 0)),
            pl.BlockSpec((D_MODEL, tn), lambda j: (0, j)),
            pl.BlockSpec((1, tn), lambda j: (0, j)),
        ],
        out_specs=[
            pl.BlockSpec((n_p, tn), lambda j: (0, j)),
            pl.BlockSpec((n_s, tn), lambda j: (0, j)),
        ],
        out_shape=[
            jax.ShapeDtypeStruct((n_p, 6 * D_MODEL), F32),
            jax.ShapeDtypeStruct((n_s, 6 * D_MODEL), F32),
        ],
        compiler_params=pltpu.CompilerParams(
            dimension_semantics=("arbitrary",), vmem_limit_bytes=VMEM_LIMIT),
        name="ada_mod",
    )(c_p, c_s, w_ada, b_ada)


def _cumsum_lanes(x, upper):
    hi = x.astype(BF16)
    r1 = x - hi.astype(F32)
    mid = r1.astype(BF16)
    lo = (r1 - mid.astype(F32)).astype(BF16)
    y = _dot(jnp.concatenate([hi, mid, lo], axis=0), upper)
    n = x.shape[0]
    return y[0:n] + y[n:2 * n] + y[2 * n:3 * n]


def _mix_kernel(x_ref, mod_ref, gpre_ref, gpost_ref, winT_hbm, wgate_ref, gb_ref, ghead_ref,
                wpool_ref, pscale_ref, wout_hbm,
                x1_ref, c_ref, n_ref, m_ref, pool_ref,
                wtm_s, wkt_s, wgt_s, wout_s, stage, sem, cn_s, m_s, hist_s):
    tl = MIX_TOKENS
    lc = CHUNK
    n_chunks = tl // lc
    lane_tiles = lc // HEAD_DIM
    b = pl.program_id(0)
    t = pl.program_id(1)
    nt = pl.num_programs(1)

    @pl.when((b == 0) & (t == 0))
    def _():
        half = D_MODEL // 2
        srcs = [winT_hbm.at[pl.ds(c, MLSTM_WIDTH), :] for c in (_CQ, _CK, _CV, _CO, _CU)]
        srcs += [wout_hbm.at[0, pl.ds(r, half), :] for r in (0, half)]
        tm_col = {0: 0, 2: MLSTM_WIDTH, 3: 2 * MLSTM_WIDTH, 4: 3 * MLSTM_WIDTH}

        def sink(i, blk):
            if i == 1:
                wkt_s[...] = blk.astype(BF16)
            elif i in tm_col:
                wtm_s[:, tm_col[i]:tm_col[i] + MLSTM_WIDTH] = blk.T.astype(BF16)
            else:
                wout_s[(i - 5) * half:(i - 4) * half, :] = blk.astype(BF16)

        _stream_cast(srcs, stage, sem, sink)
        pad = jnp.zeros((GATE_ROWS - 2 * N_HEADS, D_MODEL), F32)
        wgt_s[...] = jnp.concatenate([wgate_ref[...], pad], axis=0).astype(BF16)

    @pl.when(t == 0)
    def _():
        cn_s[...] = jnp.zeros_like(cn_s)
        m_s[...] = jnp.zeros_like(m_s)
        hist_s[...] = jnp.zeros_like(hist_s)

    mod = mod_ref[...]
    sh1 = mod[:, 0:D_MODEL]
    sc1 = mod[:, D_MODEL:2 * D_MODEL]
    ga1 = mod[:, 2 * D_MODEL:3 * D_MODEL]
    gpre = gpre_ref[...]
    gpost = gpost_ref[...]
    ghead = ghead_ref[...]
    pscale = pscale_ref[...]
    row_i = lax.broadcasted_iota(jnp.int32, (lc, lc), 0)
    col_i = lax.broadcasted_iota(jnp.int32, (lc, lc), 1)
    causal = col_i <= row_i
    upper = (row_i <= col_i).astype(BF16)
    ones_blk = jnp.ones((lc, HEAD_DIM), BF16)
    g = POOL_GROUP_DIM

    def norm_and_gates(c):
        x = x_ref[c * lc:(c + 1) * lc, :]
        hnb = ((_rms(x) * gpre) * (1.0 + sc1) + sh1).astype(BF16)
        gt = _dot_nt(wgt_s[...], hnb) + gb_ref[...]
        b16 = _cumsum_lanes(jax.nn.log_sigmoid(gt), upper)
        bcs = b16[N_HEADS:2 * N_HEADS]
        return hnb, gt[0:N_HEADS] - bcs, bcs, b16.T

    def project(hnb):
        ztm = _dot(hnb, wtm_s[...])
        kt_all = _dot_nt(wkt_s[...], hnb)
        return ztm, kt_all

    cns = [cn_s[h] for h in range(N_HEADS)]
    ms = [m_s[h] for h in range(N_HEADS)]
    hist = [hist_s[...]]
    last_ext = [None]

    def heads(ztm, kt_all, rr, bcs, bcs_col):
        outs = []
        for h in range(N_HEADS):
            sl = slice(h * HEAD_DIM, (h + 1) * HEAD_DIM)
            q = ztm[:, sl]
            v = ztm[:, MLSTM_WIDTH + h * HEAD_DIM:MLSTM_WIDTH + (h + 1) * HEAD_DIM]
            o = ztm[:, 2 * MLSTM_WIDTH + h * HEAD_DIM:2 * MLSTM_WIDTH + (h + 1) * HEAD_DIM]
            kt = kt_all[sl, :] * K_SCALE
            r_row = rr[h:h + 1, :]
            m0 = ms[h]
            cn = cns[h]

            rmat = jnp.where(causal, r_row, -jnp.inf)
            mcol = jnp.maximum(jnp.max(rmat, axis=-1, keepdims=True), m0)
            mcol_b = jnp.broadcast_to(mcol, (lc, HEAD_DIM))
            wmat = jnp.exp(rmat - jnp.concatenate([mcol_b] * lane_tiles, axis=1))
            a_b = jnp.exp(m0 - mcol_b)
            bcol = bcs_col[:, N_HEADS + h:N_HEADS + h + 1]

            qb = q.astype(BF16)
            vext = jnp.concatenate([v.astype(BF16), ones_blk], axis=1)
            s = _dot(qb, kt.astype(BF16)) * wmat
            qcn = _dot(qb, cn.astype(BF16))
            sv = _dot(s.astype(BF16), vext)
            nd = jnp.concatenate([a_b, a_b], axis=1) * qcn + sv
            num = nd[:, 0:HEAD_DIM]
            den = nd[:, HEAD_DIM:]
            hh = num / jnp.maximum(jnp.abs(den), jnp.exp(-(bcol + mcol)))
            hh = (_rms(hh) * ghead) * jax.nn.sigmoid(o)
            outs.append(hh.astype(BF16))

            ml = mcol[lc - 1:lc, :]
            al = jnp.exp(m0 - ml)
            wl = jnp.exp(r_row - ml)
            kw = (kt * wl).astype(BF16)
            cns[h] = al * cn + _dot(kw, vext)
            ms[h] = bcs[h:h + 1, lc - 1:lc] + ml
        return jnp.concatenate(outs, axis=1)

    def pool(c, u):
        ext = jnp.concatenate([hist[0], u], axis=0)
        p2 = ext + pltpu.roll(ext, 1, axis=0)
        p4 = p2[:, g:] + pltpu.roll(p2[:, g:], 2, axis=0)
        p8 = p4[:, g:] + pltpu.roll(p4[:, g:], 4, axis=0)
        p16 = p8[:, g:] + pltpu.roll(p8[:, g:], 8, axis=0)
        wsum = (p2[:, 0:g], p4[:, 0:g], p8[:, 0:g], p16)
        pos = (t * tl + c * lc + lax.broadcasted_iota(jnp.int32, (lc, 1), 0)).astype(F32)
        pouts = []
        for gi, win in enumerate(POOL_WINDOWS):
            cnt = jnp.minimum(pos + 1.0, float(win))
            ug = u[:, gi * g:(gi + 1) * g]
            pooled = wsum[gi][POOL_HIST:, :] / cnt - ug
            pouts.append(_dot(pooled.astype(BF16), wpool_ref[gi].astype(BF16)))
        hist[0] = ext[lc:, :]
        last_ext[0] = ext
        return (jnp.concatenate(pouts, axis=1) * pscale).astype(BF16)

    def finish(c, mixh, p_out):
        mix = _dot(mixh, wout_s[0:MLSTM_WIDTH, :]) + _dot(p_out, wout_s[MLSTM_WIDTH:, :])
        x = x_ref[c * lc:(c + 1) * lc, :]
        x1_ref[c * lc:(c + 1) * lc, :] = x + ga1 * (_rms(mix) * gpost)

    pres = []
    zs = []
    for c in range(n_chunks):
        pres.append(norm_and_gates(c))
        zs.append(project(pres[c][0]))
    mixhs = [heads(zs[c][0], zs[c][1], *pres[c][1:]) for c in range(n_chunks)]
    p_outs = [pool(c, zs[c][0][:, 3 * MLSTM_WIDTH:3 * MLSTM_WIDTH + POOL_WIDTH]) for c in range(n_chunks)]
    for c in range(n_chunks):
        finish(c, mixhs[c], p_outs[c])

    for h in range(N_HEADS):
        cn_s[h] = cns[h]
        m_s[h] = ms[h]
    hist_s[...] = hist[0]

    @pl.when(t == nt - 1)
    def _():
        for h in range(N_HEADS):
            c_ref[h] = cns[h][:, 0:HEAD_DIM]
            n_ref[h] = cns[h][:, HEAD_DIM:HEAD_DIM + 1]
            m_ref[h] = ms[h]
        pool_ref[...] = last_ext[0][lc + POOL_HIST - POOL_BUF:, :]


def _mix_prompt(x, mod_p, g_pre1, g_post1, w_in_t, w_gate_t, gate_bias, g_head, w_pool, pool_scale, w_out):
    bsz, seq, _ = x.shape
    tl = MIX_TOKENS
    nt = seq // tl
    const2 = lambda b, t: (0, 0)
    const3 = lambda b, t: (0, 0, 0)
    once = pl.Buffered(1)
    return pl.pallas_call(
        _mix_kernel,
        grid=(bsz, nt),
        in_specs=[
            pl.BlockSpec((None, tl, D_MODEL), lambda b, t: (b, t, 0)),
            pl.BlockSpec((None, 1, 6 * D_MODEL), lambda b, t: (b, 0, 0)),
            pl.BlockSpec((1, D_MODEL), const2),
            pl.BlockSpec((1, D_MODEL), const2),
            pl.BlockSpec(memory_space=pl.ANY),
            pl.BlockSpec(w_gate_t.shape, const2, pipeline_mode=once),
            pl.BlockSpec((GATE_ROWS, 1), const2),
            pl.BlockSpec((1, HEAD_DIM), const2),
            pl.BlockSpec(w_pool.shape, const3),
            pl.BlockSpec((1, POOL_WIDTH), const2),
            pl.BlockSpec(memory_space=pl.ANY),
        ],
        out_specs=[
            pl.BlockSpec((None, tl, D_MODEL), lambda b, t: (b, t, 0)),
            pl.BlockSpec((None, N_HEADS, HEAD_DIM, HEAD_DIM), lambda b, t: (b, 0, 0, 0)),
            pl.BlockSpec((None, N_HEADS, HEAD_DIM, 1), lambda b, t: (b, 0, 0, 0)),
            pl.BlockSpec((None, N_HEADS, 1, 1), lambda b, t: (b, 0, 0, 0)),
            pl.BlockSpec((None, POOL_BUF, POOL_WIDTH), lambda b, t: (b, 0, 0)),
        ],
        out_shape=[
            jax.ShapeDtypeStruct((bsz, seq, D_MODEL), F32),
            jax.ShapeDtypeStruct((bsz, N_HEADS, HEAD_DIM, HEAD_DIM), F32),
            jax.ShapeDtypeStruct((bsz, N_HEADS, HEAD_DIM, 1), F32),
            jax.ShapeDtypeStruct((bsz, N_HEADS, 1, 1), F32),
            jax.ShapeDtypeStruct((bsz, POOL_BUF, POOL_WIDTH), F32),
        ],
        scratch_shapes=[
            pltpu.VMEM((D_MODEL, 3 * MLSTM_WIDTH + POOL_WIDTH), BF16),
            pltpu.VMEM((MLSTM_WIDTH, D_MODEL), BF16),
            pltpu.VMEM((GATE_ROWS, D_MODEL), BF16),
            pltpu.VMEM((D_MODEL, D_MODEL), BF16),
            pltpu.VMEM((2, MLSTM_WIDTH, D_MODEL), F32),
            pltpu.SemaphoreType.DMA((2,)),
            pltpu.VMEM((N_HEADS, HEAD_DIM, 2 * HEAD_DIM), F32),
            pltpu.VMEM((N_HEADS, 1, 1), F32),
            pltpu.VMEM((POOL_HIST, POOL_WIDTH), F32),
        ],
        compiler_params=pltpu.CompilerParams(
            dimension_semantics=("arbitrary", "arbitrary"), vmem_limit_bytes=VMEM_LIMIT),
        name="mix_prompt",
    )(x, mod_p, g_pre1, g_post1, w_in_t, w_gate_t, gate_bias, g_head, w_pool, pool_scale, w_out)


def _mlp_rows(x_ref, y_ref, mod, gpre, gpost, wup_s, wdn_s, sub):
    sh2 = mod[:, 3 * D_MODEL:4 * D_MODEL]
    sc2 = mod[:, 4 * D_MODEL:5 * D_MODEL]
    ga2 = mod[:, 5 * D_MODEL:6 * D_MODEL]
    n_sub = x_ref.shape[0] // sub
    per_row = mod.shape[0] > 1

    def rows(r, a):
        return a[r * sub:(r + 1) * sub] if per_row else a

    hbs = []
    for r in range(n_sub):
        x = x_ref[r * sub:(r + 1) * sub, :]
        hbs.append(((_rms(x) * gpre) * (1.0 + rows(r, sc2)) + rows(r, sh2)).astype(BF16))
    accs = [jnp.zeros((sub, D_MODEL), F32) for _ in range(n_sub)]
    n_ff = D_FF // FF_CHUNK
    for j in range(n_ff):
        for r in range(n_sub):
            f = _dot(hbs[r], wup_s[:, j * FF_CHUNK:(j + 1) * FF_CHUNK])
            f = jnp.square(jnp.maximum(f, 0.0))
            accs[r] = accs[r] + _dot(f.astype(BF16), wdn_s[j * FF_CHUNK:(j + 1) * FF_CHUNK, :])
            if j == n_ff - 1:
                x = x_ref[r * sub:(r + 1) * sub, :]
                y_ref[r * sub:(r + 1) * sub, :] = x + rows(r, ga2) * (_rms(accs[r]) * gpost)


def _mlp_kernel(xp_ref, xs_ref, modp_ref, mods_ref, gpre_ref, gpost_ref, wup_hbm, wdn_hbm,
                yp_ref, ys_ref, wup_s, wdn_s, stage_u, stage_d, sem):
    i = pl.program_id(0)
    n_prompt = pl.num_programs(0) - 1

    @pl.when(i == 0)
    def _():
        ru = stage_u.shape[1]
        rd = stage_d.shape[1]

        def sink_u(k, blk):
            wup_s[k * ru:(k + 1) * ru, :] = blk.astype(BF16)

        def sink_d(k, blk):
            wdn_s[k * rd:(k + 1) * rd, :] = blk.astype(BF16)

        _stream_cast([wup_hbm.at[0, pl.ds(k * ru, ru), :] for k in range(D_MODEL // ru)], stage_u, sem, sink_u)
        _stream_cast([wdn_hbm.at[0, pl.ds(k * rd, rd), :] for k in range(D_FF // rd)], stage_d, sem, sink_d)

    @pl.when(i < n_prompt)
    def _():
        _mlp_rows(xp_ref, yp_ref, modp_ref[...], gpre_ref[...], gpost_ref[...], wup_s, wdn_s, MLP_SUB)

    @pl.when(i == n_prompt)
    def _():
        _mlp_rows(xs_ref, ys_ref, mods_ref[...], gpre_ref[...], gpost_ref[...], wup_s, wdn_s, xs_ref.shape[0])


def _mlp(xp, xs, mod_p, mod_s, rows_per_mod, g_pre2, g_post2, w_up, w_dn):
    n_p = xp.shape[0]
    n_s = xs.shape[0]
    tm = MLP_ROWS
    n_tiles = n_p // tm
    steps_per_mod = rows_per_mod // tm
    last = n_tiles - 1
    const2 = lambda i: (0, 0)
    ptile = lambda i: (jnp.minimum(i, last), 0)
    once = pl.Buffered(1)
    return pl.pallas_call(
        _mlp_kernel,
        grid=(n_tiles + 1,),
        in_specs=[
            pl.BlockSpec((tm, D_MODEL), ptile),
            pl.BlockSpec((n_s, D_MODEL), const2, pipeline_mode=once),
            pl.BlockSpec((None, 1, 6 * D_MODEL), lambda i: (jnp.minimum(i, last) // steps_per_mod, 0, 0)),
            pl.BlockSpec((n_s, 6 * D_MODEL), const2, pipeline_mode=once),
            pl.BlockSpec((1, D_MODEL), const2),
            pl.BlockSpec((1, D_MODEL), const2),
            pl.BlockSpec(memory_space=pl.ANY),
            pl.BlockSpec(memory_space=pl.ANY),
        ],
        out_specs=[
            pl.BlockSpec((tm, D_MODEL), ptile),
            pl.BlockSpec((n_s, D_MODEL), const2),
        ],
        out_shape=[
            jax.ShapeDtypeStruct((n_p, D_MODEL), F32),
            jax.ShapeDtypeStruct((n_s, D_MODEL), F32),
        ],
        scratch_shapes=[
            pltpu.VMEM((D_MODEL, D_FF), BF16),
            pltpu.VMEM((D_FF, D_MODEL), BF16),
            pltpu.VMEM((2, 64, D_FF), F32),
            pltpu.VMEM((2, 256, D_MODEL), F32),
            pltpu.SemaphoreType.DMA((2,)),
        ],
        compiler_params=pltpu.CompilerParams(
            dimension_semantics=("arbitrary",), vmem_limit_bytes=VMEM_LIMIT),
        name="mlp",
    )(xp, xs, mod_p, mod_s, g_pre2, g_post2, w_up, w_dn)


def _sproj_kernel(x_ref, mod_ref, gpre_ref, w_ref, z_ref):
    x = x_ref[...]
    mod = mod_ref[...]
    sh1 = mod[:, 0:D_MODEL]
    sc1 = mod[:, D_MODEL:2 * D_MODEL]
    hn = (_rms(x) * gpre_ref[...]) * (1.0 + sc1) + sh1
    z_ref[...] = _dot_nt(hn.astype(BF16), w_ref[...].astype(BF16))


def _sample_proj(x, mod_s, g_pre1, w_in_t):
    n = x.shape[0]
    return pl.pallas_call(
        _sproj_kernel,
        out_shape=jax.ShapeDtypeStruct((n, IN_COLS), F32),
        compiler_params=pltpu.CompilerParams(vmem_limit_bytes=VMEM_LIMIT),
        name="sample_proj",
    )(x, mod_s, g_pre1, w_in_t)


def _sample_gates(gates, m0):
    ig = gates[:, 0:N_HEADS]
    logf = jax.nn.log_sigmoid(gates[:, N_HEADS:2 * N_HEADS])
    g = logf + m0
    m = jnp.maximum(g, ig)
    return m, jnp.exp(ig - m), jnp.exp(g - m)


def _sstate_kernel(z_ref, m0_ref, gb_ref, c0_ref, c1_ref, qc_ref):
    z = z_ref[...]
    gates = z[:, _CG:_CG + 2 * N_HEADS] + gb_ref[...]
    _, w, a = _sample_gates(gates, m0_ref[...])
    sub = lax.broadcasted_iota(jnp.int32, (8, HEAD_DIM), 0)
    for j in range(STATE_TOKENS):
        rows = []
        for h in range(N_HEADS):
            q = z[j:j + 1, _CQ + h * HEAD_DIM:_CQ + (h + 1) * HEAD_DIM]
            k = z[j:j + 1, _CK + h * HEAD_DIM:_CK + (h + 1) * HEAD_DIM] * K_SCALE
            v = z[j:j + 1, _CV + h * HEAD_DIM:_CV + (h + 1) * HEAD_DIM]
            ah = a[j:j + 1, h:h + 1]
            wh = w[j:j + 1, h:h + 1]
            c0 = c0_ref[j, h]
            q8 = jnp.broadcast_to(q, (8, HEAD_DIM)).astype(BF16)
            rows.append(_dot(q8, c0.astype(BF16))[0:1, :])
            kw8 = jnp.where(sub == 0, jnp.broadcast_to(k * wh, (8, HEAD_DIM)), 0.0).astype(BF16)
            v8 = jnp.broadcast_to(v, (8, HEAD_DIM)).astype(BF16)
            c1_ref[j, h] = ah * c0 + _dot_tn(kw8, v8)
        qc_ref[j:j + 1, :] = jnp.concatenate(rows, axis=1)


def _sample_state(z_s, m0, gate_bias_row, c0):
    n = z_s.shape[0]
    bt = STATE_TOKENS
    return pl.pallas_call(
        _sstate_kernel,
        grid=(n // bt,),
        in_specs=[
            pl.BlockSpec((bt, z_s.shape[1]), lambda i: (i, 0)),
            pl.BlockSpec((bt, N_HEADS), lambda i: (i, 0)),
            pl.BlockSpec((1, 2 * N_HEADS), lambda i: (0, 0)),
            pl.BlockSpec((bt, N_HEADS, HEAD_DIM, HEAD_DIM), lambda i: (i, 0, 0, 0)),
        ],
        out_specs=[
            pl.BlockSpec((bt, N_HEADS, HEAD_DIM, HEAD_DIM), lambda i: (i, 0, 0, 0)),
            pl.BlockSpec((bt, MLSTM_WIDTH), lambda i: (i, 0)),
        ],
        out_shape=[
            jax.ShapeDtypeStruct(c0.shape, F32),
            jax.ShapeDtypeStruct((n, MLSTM_WIDTH), F32),
        ],
        compiler_params=pltpu.CompilerParams(
            dimension_semantics=("arbitrary",), vmem_limit_bytes=VMEM_LIMIT),
        name="sample_state",
    )(z_s, m0, gate_bias_row, c0)


def _smix_kernel(x_ref, z_ref, qc_ref, mod_ref, n0_ref, m0_ref, pool_ref, gb_ref, ghead_ref,
                 wpool_ref, pscale_ref, wout_ref, gpost_ref,
                 x1_ref, n1_ref, m1_ref, poolo_ref):
    x = x_ref[...]
    z = z_ref[...]
    mod = mod_ref[...]
    ga1 = mod[:, 2 * D_MODEL:3 * D_MODEL]
    gates = z[:, _CG:_CG + 2 * N_HEADS] + gb_ref[...]
    m, w, a = _sample_gates(gates, m0_ref[...])
    m1_ref[...] = m
    ghead = ghead_ref[...]
    heads = []
    for h in range(N_HEADS):
        sl = slice(h * HEAD_DIM, (h + 1) * HEAD_DIM)
        q = z[:, _CQ + h * HEAD_DIM:_CQ + (h + 1) * HEAD_DIM]
        k = z[:, _CK + h * HEAD_DIM:_CK + (h + 1) * HEAD_DIM] * K_SCALE
        v = z[:, _CV + h * HEAD_DIM:_CV + (h + 1) * HEAD_DIM]
        o = z[:, _CO + h * HEAD_DIM:_CO + (h + 1) * HEAD_DIM]
        n0 = n0_ref[:, sl]
        ah = a[:, h:h + 1]
        wh = w[:, h:h + 1]
        s = jnp.sum(q * k, axis=-1, keepdims=True) * wh
        num = ah * qc_ref[:, sl] + s * v
        den = ah * jnp.sum(q * n0, axis=-1, keepdims=True) + s
        hh = num / jnp.maximum(jnp.abs(den), jnp.exp(-m[:, h:h + 1]))
        n1_ref[:, sl] = ah * n0 + wh * k
        hh = (_rms(hh) * ghead) * jax.nn.sigmoid(o)
        heads.append(hh.astype(BF16))

    u = z[:, _CU:_CU + POOL_WIDTH]
    g = POOL_GROUP_DIM
    pouts = []
    for gi, win in enumerate(POOL_WINDOWS):
        ug = u[:, gi * g:(gi + 1) * g]
        wsum = ug
        for r in range(POOL_BUF - (win - 1), POOL_BUF):
            wsum = wsum + pool_ref[r, :, gi * g:(gi + 1) * g]
        cnt = min(PAST_LEN + 1.0, float(win))
        pooled = wsum / cnt - ug
        pouts.append(_dot(pooled.astype(BF16), wpool_ref[gi].astype(BF16)))
    p_out = jnp.concatenate(pouts, axis=1) * pscale_ref[...]
    poolo_ref[0:POOL_BUF - 1] = pool_ref[1:POOL_BUF]
    poolo_ref[POOL_BUF - 1] = u

    mixh = jnp.concatenate(heads, axis=1)
    mix = (_dot(mixh, wout_ref[0:MLSTM_WIDTH, :].astype(BF16))
           + _dot(p_out.astype(BF16), wout_ref[MLSTM_WIDTH:, :].astype(BF16)))
    x1_ref[...] = x + ga1 * (_rms(mix) * gpost_ref[...])


def _sample_mix(x, z_s, qc, mod_s, n0, m0, pool_rows, gate_bias_row, g_head, w_pool, pool_scale, w_out, g_post1):
    n = x.shape[0]
    return pl.pallas_call(
        _smix_kernel,
        out_shape=[
            jax.ShapeDtypeStruct((n, D_MODEL), F32),
            jax.ShapeDtypeStruct((n, MLSTM_WIDTH), F32),
            jax.ShapeDtypeStruct((n, N_HEADS), F32),
            jax.ShapeDtypeStruct(pool_rows.shape, F32),
        ],
        compiler_params=pltpu.CompilerParams(vmem_limit_bytes=VMEM_LIMIT),
        name="sample_mix",
    )(x, z_s, qc, mod_s, n0, m0, pool_rows, gate_bias_row, g_head, w_pool, pool_scale, w_out, g_post1)


def kernel(x_prompt, x_sample, c_prompt, c_sample, state_C, state_n, state_m, state_pool, w_ada, b_ada,
           g_pre1, g_post1, w_in, b_ig, b_fg, g_head, w_pool, pool_scale, w_out, g_pre2, g_post2,
           w_up, w_down):
    depth = w_ada.shape[0]
    assert depth == 1, "single-layer step"
    bsz, seq, _ = x_prompt.shape
    nb = x_sample.shape[0]
    assert x_sample.shape[1] == 1 and seq % MIX_TOKENS == 0 and seq % MLP_ROWS == 0 and nb % STATE_TOKENS == 0
    assert w_in.shape[2] == IN_COLS
    l = 0

    gate_bias = jnp.concatenate([b_ig[l], b_fg[l]])
    gate_bias_col = jnp.pad(gate_bias, (0, GATE_ROWS - 2 * N_HEADS))[:, None]
    gate_bias_row = gate_bias[None, :]
    gpre1 = g_pre1[l][None, :]
    gpost1 = g_post1[l][None, :]
    gpre2 = g_pre2[l][None, :]
    gpost2 = g_post2[l][None, :]
    ghead = g_head[l][None, :]
    pscale = pool_scale[l][None, :]
    w_in_t = jnp.transpose(w_in[l])
    w_gate_t = w_in_t[_CG:_CG + 2 * N_HEADS]

    mod_p, mod_s = _ada(c_prompt, c_sample, w_ada[l], b_ada[l][None, :])
    mod_p = mod_p.reshape(bsz, 1, 6 * D_MODEL)

    x1p, c_p, n_p, m_p, pool_p = _mix_prompt(
        x_prompt, mod_p, gpre1, gpost1, w_in_t, w_gate_t, gate_bias_col, ghead, w_pool[l], pscale, w_out)

    xs = x_sample.reshape(nb, D_MODEL)
    m0 = state_m[l]
    z_s = _sample_proj(xs, mod_s, gpre1, w_in_t)
    c_s, qc = _sample_state(z_s, m0, gate_bias_row, state_C[l])
    x1s, n_s, m_s, pool_s = _sample_mix(
        xs, z_s, qc, mod_s, state_n[l].reshape(nb, MLSTM_WIDTH), m0,
        jnp.transpose(state_pool[l], (1, 0, 2)), gate_bias_row, ghead, w_pool[l], pscale,
        w_out[l], gpost1)

    y_p, y_s = _mlp(x1p.reshape(bsz * seq, D_MODEL), x1s, mod_p, mod_s, seq, gpre2, gpost2, w_up, w_down)

    return (y_p.reshape(bsz, seq, D_MODEL), y_s.reshape(nb, 1, D_MODEL),
            c_p[None], n_p.reshape(1, bsz, N_HEADS, HEAD_DIM), m_p.reshape(1, bsz, N_HEADS), pool_p[None],
            c_s[None], n_s.reshape(1, nb, N_HEADS, HEAD_DIM), m_s[None],
            jnp.transpose(pool_s, (1, 0, 2))[None])
```

```python
import jax
import jax.numpy as jnp
from jax import lax
from jax.experimental import pallas as pl
from jax.experimental.pallas import tpu as pltpu

F32 = jnp.float32
BF16 = jnp.bfloat16

D_MODEL = 1024
N_HEADS = 4
HEAD_DIM = 128
MLSTM_WIDTH = N_HEADS * HEAD_DIM
POOL_WIDTH = 512
POOL_WINDOWS = (2, 4, 8, 16)
POOL_GROUP_DIM = 128
POOL_BUF = 15
POOL_HIST = 16
D_FF = 4 * D_MODEL
EPS = 1e-6
PAST_LEN = 16384
K_SCALE = HEAD_DIM ** -0.5
GATE_ROWS = 16

_CQ, _CK, _CV, _CO = 0, 512, 1024, 1536
_CG = 2048
_CU = _CG + 2 * N_HEADS
IN_COLS = _CU + POOL_WIDTH

MIX_TOKENS = 1024
CHUNK = 256
MLP_ROWS = 1024
MLP_SUB = 512
FF_CHUNK = 1024
STATE_TOKENS = 16
ADA_COLS = 1024
STAGE_SLOTS = 3
VMEM_LIMIT = 60 * 1024 * 1024


def _dot(a, b):
    return jnp.dot(a, b, preferred_element_type=F32)


def _dot_nt(a, b):
    return lax.dot_general(a, b, (((1,), (1,)), ((), ())), preferred_element_type=F32)


def _dot_tn(a, b):
    return lax.dot_general(a, b, (((0,), (0,)), ((), ())), preferred_element_type=F32)


def _rms(x):
    return x * lax.rsqrt(jnp.mean(x * x, axis=-1, keepdims=True) + EPS)


def _stream_cast(srcs, stage, sem, sink):
    n_slots = stage.shape[0]
    ahead = n_slots - 1

    def copy(i):
        return pltpu.make_async_copy(srcs[i], stage.at[i % n_slots], sem.at[i % n_slots])

    for i in range(min(ahead, len(srcs))):
        copy(i).start()
    for i in range(len(srcs)):
        if i + ahead < len(srcs):
            copy(i + ahead).start()
        copy(i).wait()
        sink(i, stage[i % n_slots])


def _ada_kernel(cp_ref, cs_ref, w_ref, b_ref, op_ref, os_ref):
    c = jnp.concatenate([cp_ref[...], cs_ref[...]], axis=0)
    s = c * jax.nn.sigmoid(c)
    mod = _dot(s.astype(BF16), w_ref[...].astype(BF16)) + b_ref[...]
    n_p = cp_ref.shape[0]
    op_ref[...] = mod[0:n_p]
    os_ref[...] = mod[n_p:]


def _ada(c_p, c_s, w_ada, b_ada):
    n_p = c_p.shape[0]
    n_s = c_s.shape[0]
    tn = ADA_COLS
    return pl.pallas_call(
        _ada_kernel,
        grid=(6 * D_MODEL // tn,),
        in_specs=[
            pl.BlockSpec((n_p, D_MODEL), lambda j: (0, 0)),
            pl.BlockSpec((n_s, D_MODEL), lambda j: (0, 0)),
            pl.BlockSpec((D_MODEL, tn), lambda j: (0, j)),
            pl.BlockSpec((1, tn), lambda j: (0, j)),
        ],
        out_specs=[
            pl.BlockSpec((n_p, tn), lambda j: (0, j)),
            pl.BlockSpec((n_s, tn), lambda j: (0, j)),
        ],
        out_shape=[
            jax.ShapeDtypeStruct((n_p, 6 * D_MODEL), F32),
            jax.ShapeDtypeStruct((n_s, 6 * D_MODEL), F32),
        ],
        compiler_params=pltpu.CompilerParams(
            dimension_semantics=("arbitrary",), vmem_limit_bytes=VMEM_LIMIT),
        name="ada_mod",
    )(c_p, c_s, w_ada, b_ada)


def _cumsum_lanes(x, upper):
    hi = x.astype(BF16)
    r1 = x - hi.astype(F32)
    mid = r1.astype(BF16)
    lo = (r1 - mid.astype(F32)).astype(BF16)
    y = _dot(jnp.concatenate([hi, mid, lo], axis=0), upper)
    n = x.shape[0]
    return y[0:n] + y[n:2 * n] + y[2 * n:3 * n]


def _mix_kernel(x_ref, mod_ref, gpre_ref, gpost_ref, winT_hbm, wgate_ref, gb_ref, ghead_ref,
                wpool_ref, pscale_ref, wout_hbm,
                x1_ref, c_ref, n_ref, m_ref, pool_ref,
                wtm_s, wkt_s, wgt_s, wout_s, stage, sem, cn_s, m_s, hist_s):
    tl = MIX_TOKENS
    lc = CHUNK
    n_chunks = tl // lc
    lane_tiles = lc // HEAD_DIM
    b = pl.program_id(0)
    t = pl.program_id(1)
    nt = pl.num_programs(1)

    @pl.when((b == 0) & (t == 0))
    def _():
        half = D_MODEL // 2
        srcs = [winT_hbm.at[pl.ds(c, MLSTM_WIDTH), :] for c in (_CQ, _CK, _CV, _CO, _CU)]
        srcs += [wout_hbm.at[0, pl.ds(r, half), :] for r in (0, half)]
        tm_col = {0: 0, 2: MLSTM_WIDTH, 3: 2 * MLSTM_WIDTH, 4: 3 * MLSTM_WIDTH}

        def sink(i, blk):
            if i == 1:
                wkt_s[...] = blk.astype(BF16)
            elif i in tm_col:
                wtm_s[:, tm_col[i]:tm_col[i] + MLSTM_WIDTH] = blk.T.astype(BF16)
            else:
                wout_s[(i - 5) * half:(i - 4) * half, :] = blk.astype(BF16)

        _stream_cast(srcs, stage, sem, sink)
        pad = jnp.zeros((GATE_ROWS - 2 * N_HEADS, D_MODEL), F32)
        wgt_s[...] = jnp.concatenate([wgate_ref[...], pad], axis=0).astype(BF16)

    @pl.when(t == 0)
    def _():
        cn_s[...] = jnp.zeros_like(cn_s)
        m_s[...] = jnp.zeros_like(m_s)
        hist_s[...] = jnp.zeros_like(hist_s)

    mod = mod_ref[...]
    sh1 = mod[:, 0:D_MODEL]
    sc1 = mod[:, D_MODEL:2 * D_MODEL]
    ga1 = mod[:, 2 * D_MODEL:3 * D_MODEL]
    gpre = gpre_ref[...]
    gpost = gpost_ref[...]
    ghead = ghead_ref[...]
    pscale = pscale_ref[...]
    row_i = lax.broadcasted_iota(jnp.int32, (lc, lc), 0)
    col_i = lax.broadcasted_iota(jnp.int32, (lc, lc), 1)
    causal = col_i <= row_i
    upper = (row_i <= col_i).astype(BF16)
    ones_blk = jnp.ones((lc, HEAD_DIM), BF16)
    g = POOL_GROUP_DIM

    def norm_and_gates(c):
        x = x_ref[c * lc:(c + 1) * lc, :]
        hnb = ((_rms(x) * gpre) * (1.0 + sc1) + sh1).astype(BF16)
        gt = _dot_nt(wgt_s[...], hnb) + gb_ref[...]
        b16 = _cumsum_lanes(jax.nn.log_sigmoid(gt), upper)
        bcs = b16[N_HEADS:2 * N_HEADS]
        return hnb, gt[0:N_HEADS] - bcs, bcs, b16.T

    def project(hnb):
        ztm = _dot(hnb, wtm_s[...])
        kt_all = _dot_nt(wkt_s[...], hnb)
        return ztm, kt_all

    cns = [cn_s[h] for h in range(N_HEADS)]
    ms = [m_s[h] for h in range(N_HEADS)]
    hist = [hist_s[...]]
    last_ext = [None]

    def heads(ztm, kt_all, rr, bcs, bcs_col):
        outs = []
        for h in range(N_HEADS):
            sl = slice(h * HEAD_DIM, (h + 1) * HEAD_DIM)
            q = ztm[:, sl]
            v = ztm[:, MLSTM_WIDTH + h * HEAD_DIM:MLSTM_WIDTH + (h + 1) * HEAD_DIM]
            o = ztm[:, 2 * MLSTM_WIDTH + h * HEAD_DIM:2 * MLSTM_WIDTH + (h + 1) * HEAD_DIM]
            kt = kt_all[sl, :] * K_SCALE
            r_row = rr[h:h + 1, :]
            m0 = ms[h]
            cn = cns[h]

            rmat = jnp.where(causal, r_row, -jnp.inf)
            mcol = jnp.maximum(jnp.max(rmat, axis=-1, keepdims=True), m0)
            mcol_b = jnp.broadcast_to(mcol, (lc, HEAD_DIM))
            wmat = jnp.exp(rmat - jnp.concatenate([mcol_b] * lane_tiles, axis=1))
            a_b = jnp.exp(m0 - mcol_b)
            bcol = bcs_col[:, N_HEADS + h:N_HEADS + h + 1]

            qb = q.astype(BF16)
            vext = jnp.concatenate([v.astype(BF16), ones_blk], axis=1)
            s = _dot(qb, kt.astype(BF16)) * wmat
            qcn = _dot(qb, cn.astype(BF16))
            sv = _dot(s.astype(BF16), vext)
            nd = jnp.concatenate([a_b, a_b], axis=1) * qcn + sv
            num = nd[:, 0:HEAD_DIM]
            den = nd[:, HEAD_DIM:]
            hh = num / jnp.maximum(jnp.abs(den), jnp.exp(-(bcol + mcol)))
            hh = (_rms(hh) * ghead) * jax.nn.sigmoid(o)
            outs.append(hh.astype(BF16))

            ml = mcol[lc - 1:lc, :]
            al = jnp.exp(m0 - ml)
            wl = jnp.exp(r_row - ml)
            kw = (kt * wl).astype(BF16)
            cns[h] = al * cn + _dot(kw, vext)
            ms[h] = bcs[h:h + 1, lc - 1:lc] + ml
        return jnp.concatenate(outs, axis=1)

    def pool(c, u):
        ext = jnp.concatenate([hist[0], u], axis=0)
        p2 = ext + pltpu.roll(ext, 1, axis=0)
        p4 = p2[:, g:] + pltpu.roll(p2[:, g:], 2, axis=0)
        p8 = p4[:, g:] + pltpu.roll(p4[:, g:], 4, axis=0)
        p16 = p8[:, g:] + pltpu.roll(p8[:, g:], 8, axis=0)
        wsum = (p2[:, 0:g], p4[:, 0:g], p8[:, 0:g], p16)
        pos = (t * tl + c * lc + lax.broadcasted_iota(jnp.int32, (lc, 1), 0)).astype(F32)
        pouts = []
        for gi, win in enumerate(POOL_WINDOWS):
            cnt = jnp.minimum(pos + 1.0, float(win))
            ug = u[:, gi * g:(gi + 1) * g]
            pooled = wsum[gi][POOL_HIST:, :] / cnt - ug
            pouts.append(_dot(pooled.astype(BF16), wpool_ref[gi].astype(BF16)))
        hist[0] = ext[lc:, :]
        last_ext[0] = ext
        return (jnp.concatenate(pouts, axis=1) * pscale).astype(BF16)

    def finish(c, mixh, p_out):
        mix = _dot(mixh, wout_s[0:MLSTM_WIDTH, :]) + _dot(p_out, wout_s[MLSTM_WIDTH:, :])
        x = x_ref[c * lc:(c + 1) * lc, :]
        x1_ref[c * lc:(c + 1) * lc, :] = x + ga1 * (_rms(mix) * gpost)

    pres = []
    zs = []
    for c in range(n_chunks):
        pres.append(norm_and_gates(c))
        zs.append(project(pres[c][0]))
    mixhs = [heads(zs[c][0], zs[c][1], *pres[c][1:]) for c in range(n_chunks)]
    p_outs = [pool(c, zs[c][0][:, 3 * MLSTM_WIDTH:3 * MLSTM_WIDTH + POOL_WIDTH]) for c in range(n_chunks)]
    for c in range(n_chunks):
        finish(c, mixhs[c], p_outs[c])

    for h in range(N_HEADS):
        cn_s[h] = cns[h]
        m_s[h] = ms[h]
    hist_s[...] = hist[0]

    @pl.when(t == nt - 1)
    def _():
        for h in range(N_HEADS):
            c_ref[h] = cns[h][:, 0:HEAD_DIM]
            n_ref[h] = cns[h][:, HEAD_DIM:HEAD_DIM + 1]
            m_ref[h] = ms[h]
        pool_ref[...] = last_ext[0][lc + POOL_HIST - POOL_BUF:, :]


def _mix_prompt(x, mod_p, g_pre1, g_post1, w_in_t, w_gate_t, gate_bias, g_head, w_pool, pool_scale, w_out):
    bsz, seq, _ = x.shape
    tl = MIX_TOKENS
    nt = seq // tl
    const2 = lambda b, t: (0, 0)
    const3 = lambda b, t: (0, 0, 0)
    once = pl.Buffered(1)
    return pl.pallas_call(
        _mix_kernel,
        grid=(bsz, nt),
        in_specs=[
            pl.BlockSpec((None, tl, D_MODEL), lambda b, t: (b, t, 0)),
            pl.BlockSpec((None, 1, 6 * D_MODEL), lambda b, t: (b, 0, 0)),
            pl.BlockSpec((1, D_MODEL), const2),
            pl.BlockSpec((1, D_MODEL), const2),
            pl.BlockSpec(memory_space=pl.ANY),
            pl.BlockSpec(w_gate_t.shape, const2, pipeline_mode=once),
            pl.BlockSpec((GATE_ROWS, 1), const2),
            pl.BlockSpec((1, HEAD_DIM), const2),
            pl.BlockSpec(w_pool.shape, const3),
            pl.BlockSpec((1, POOL_WIDTH), const2),
            pl.BlockSpec(memory_space=pl.ANY),
        ],
        out_specs=[
            pl.BlockSpec((None, tl, D_MODEL), lambda b, t: (b, t, 0)),
            pl.BlockSpec((None, N_HEADS, HEAD_DIM, HEAD_DIM), lambda b, t: (b, 0, 0, 0)),
            pl.BlockSpec((None, N_HEADS, HEAD_DIM, 1), lambda b, t: (b, 0, 0, 0)),
            pl.BlockSpec((None, N_HEADS, 1, 1), lambda b, t: (b, 0, 0, 0)),
            pl.BlockSpec((None, POOL_BUF, POOL_WIDTH), lambda b, t: (b, 0, 0)),
        ],
        out_shape=[
            jax.ShapeDtypeStruct((bsz, seq, D_MODEL), F32),
            jax.ShapeDtypeStruct((bsz, N_HEADS, HEAD_DIM, HEAD_DIM), F32),
            jax.ShapeDtypeStruct((bsz, N_HEADS, HEAD_DIM, 1), F32),
            jax.ShapeDtypeStruct((bsz, N_HEADS, 1, 1), F32),
            jax.ShapeDtypeStruct((bsz, POOL_BUF, POOL_WIDTH), F32),
        ],
        scratch_shapes=[
            pltpu.VMEM((D_MODEL, 3 * MLSTM_WIDTH + POOL_WIDTH), BF16),
            pltpu.VMEM((MLSTM_WIDTH, D_MODEL), BF16),
            pltpu.VMEM((GATE_ROWS, D_MODEL), BF16),
            pltpu.VMEM((D_MODEL, D_MODEL), BF16),
            pltpu.VMEM((STAGE_SLOTS, MLSTM_WIDTH, D_MODEL), F32),
            pltpu.SemaphoreType.DMA((STAGE_SLOTS,)),
            pltpu.VMEM((N_HEADS, HEAD_DIM, 2 * HEAD_DIM), F32),
            pltpu.VMEM((N_HEADS, 1, 1), F32),
            pltpu.VMEM((POOL_HIST, POOL_WIDTH), F32),
        ],
        compiler_params=pltpu.CompilerParams(
            dimension_semantics=("arbitrary", "arbitrary"), vmem_limit_bytes=VMEM_LIMIT),
        name="mix_prompt",
    )(x, mod_p, g_pre1, g_post1, w_in_t, w_gate_t, gate_bias, g_head, w_pool, pool_scale, w_out)


def _mlp_rows(x_ref, y_ref, mod, gpre, gpost, wup_s, wdn_s, sub):
    sh2 = mod[:, 3 * D_MODEL:4 * D_MODEL]
    sc2 = mod[:, 4 * D_MODEL:5 * D_MODEL]
    ga2 = mod[:, 5 * D_MODEL:6 * D_MODEL]
    n_sub = x_ref.shape[0] // sub
    per_row = mod.shape[0] > 1

    def rows(r, a):
        return a[r * sub:(r + 1) * sub] if per_row else a

    hbs = []
    for r in range(n_sub):
        x = x_ref[r * sub:(r + 1) * sub, :]
        hbs.append(((_rms(x) * gpre) * (1.0 + rows(r, sc2)) + rows(r, sh2)).astype(BF16))
    accs = [jnp.zeros((sub, D_MODEL), F32) for _ in range(n_sub)]
    n_ff = D_FF // FF_CHUNK
    for j in range(n_ff):
        for r in range(n_sub):
            f = _dot(hbs[r], wup_s[:, j * FF_CHUNK:(j + 1) * FF_CHUNK])
            f = jnp.square(jnp.maximum(f, 0.0))
            accs[r] = accs[r] + _dot(f.astype(BF16), wdn_s[j * FF_CHUNK:(j + 1) * FF_CHUNK, :])
            if j == n_ff - 1:
                x = x_ref[r * sub:(r + 1) * sub, :]
                y_ref[r * sub:(r + 1) * sub, :] = x + rows(r, ga2) * (_rms(accs[r]) * gpost)


def _mlp_kernel(xp_ref, xs_ref, modp_ref, mods_ref, gpre_ref, gpost_ref, wup_hbm, wdn_hbm,
                yp_ref, ys_ref, wup_s, wdn_s, stage, sem):
    i = pl.program_id(0)
    n_prompt = pl.num_programs(0) - 1

    @pl.when(i == 0)
    def _():
        rows, cols = stage.shape[1], stage.shape[2]
        up_blocks = [(r, c) for c in range(0, D_FF, cols) for r in range(0, D_MODEL, rows)]
        srcs = [wup_hbm.at[0, pl.ds(r, rows), pl.ds(c, cols)] for r, c in up_blocks]
        srcs += [wdn_hbm.at[0, pl.ds(r, rows), :] for r in range(0, D_FF, rows)]

        def sink(k, blk):
            if k < len(up_blocks):
                r, c = up_blocks[k]
                wup_s[r:r + rows, c:c + cols] = blk.astype(BF16)
            else:
                r = (k - len(up_blocks)) * rows
                wdn_s[r:r + rows, :] = blk.astype(BF16)

        _stream_cast(srcs, stage, sem, sink)

    @pl.when(i < n_prompt)
    def _():
        _mlp_rows(xp_ref, yp_ref, modp_ref[...], gpre_ref[...], gpost_ref[...], wup_s, wdn_s, MLP_SUB)

    @pl.when(i == n_prompt)
    def _():
        _mlp_rows(xs_ref, ys_ref, mods_ref[...], gpre_ref[...], gpost_ref[...], wup_s, wdn_s, xs_ref.shape[0])


def _mlp(xp, xs, mod_p, mod_s, rows_per_mod, g_pre2, g_post2, w_up, w_dn):
    n_p = xp.shape[0]
    n_s = xs.shape[0]
    tm = MLP_ROWS
    n_tiles = n_p // tm
    steps_per_mod = rows_per_mod // tm
    last = n_tiles - 1
    const2 = lambda i: (0, 0)
    ptile = lambda i: (jnp.minimum(i, last), 0)
    once = pl.Buffered(1)
    return pl.pallas_call(
        _mlp_kernel,
        grid=(n_tiles + 1,),
        in_specs=[
            pl.BlockSpec((tm, D_MODEL), ptile),
            pl.BlockSpec((n_s, D_MODEL), const2, pipeline_mode=once),
            pl.BlockSpec((None, 1, 6 * D_MODEL), lambda i: (jnp.minimum(i, last) // steps_per_mod, 0, 0)),
            pl.BlockSpec((n_s, 6 * D_MODEL), const2, pipeline_mode=once),
            pl.BlockSpec((1, D_MODEL), const2),
            pl.BlockSpec((1, D_MODEL), const2),
            pl.BlockSpec(memory_space=pl.ANY),
            pl.BlockSpec(memory_space=pl.ANY),
        ],
        out_specs=[
            pl.BlockSpec((tm, D_MODEL), ptile),
            pl.BlockSpec((n_s, D_MODEL), const2),
        ],
        out_shape=[
            jax.ShapeDtypeStruct((n_p, D_MODEL), F32),
            jax.ShapeDtypeStruct((n_s, D_MODEL), F32),
        ],
        scratch_shapes=[
            pltpu.VMEM((D_MODEL, D_FF), BF16),
            pltpu.VMEM((D_FF, D_MODEL), BF16),
            pltpu.VMEM((STAGE_SLOTS, 512, D_MODEL), F32),
            pltpu.SemaphoreType.DMA((STAGE_SLOTS,)),
        ],
        compiler_params=pltpu.CompilerParams(
            dimension_semantics=("arbitrary",), vmem_limit_bytes=VMEM_LIMIT),
        name="mlp",
    )(xp, xs, mod_p, mod_s, g_pre2, g_post2, w_up, w_dn)


def _sproj_kernel(x_ref, mod_ref, gpre_ref, w_ref, z_ref):
    x = x_ref[...]
    mod = mod_ref[...]
    sh1 = mod[:, 0:D_MODEL]
    sc1 = mod[:, D_MODEL:2 * D_MODEL]
    hn = (_rms(x) * gpre_ref[...]) * (1.0 + sc1) + sh1
    z_ref[...] = _dot_nt(hn.astype(BF16), w_ref[...].astype(BF16))


def _sample_proj(x, mod_s, g_pre1, w_in_t):
    n = x.shape[0]
    return pl.pallas_call(
        _sproj_kernel,
        out_shape=jax.ShapeDtypeStruct((n, IN_COLS), F32),
        compiler_params=pltpu.CompilerParams(vmem_limit_bytes=VMEM_LIMIT),
        name="sample_proj",
    )(x, mod_s, g_pre1, w_in_t)


def _sample_gates(gates, m0):
    ig = gates[:, 0:N_HEADS]
    logf = jax.nn.log_sigmoid(gates[:, N_HEADS:2 * N_HEADS])
    g = logf + m0
    m = jnp.maximum(g, ig)
    return m, jnp.exp(ig - m), jnp.exp(g - m)


def _sstate_kernel(z_ref, m0_ref, gb_ref, c0_ref, c1_ref, qc_ref):
    z = z_ref[...]
    gates = z[:, _CG:_CG + 2 * N_HEADS] + gb_ref[...]
    _, w, a = _sample_gates(gates, m0_ref[...])
    sub = lax.broadcasted_iota(jnp.int32, (8, HEAD_DIM), 0)
    for j in range(STATE_TOKENS):
        rows = []
        for h in range(N_HEADS):
            q = z[j:j + 1, _CQ + h * HEAD_DIM:_CQ + (h + 1) * HEAD_DIM]
            k = z[j:j + 1, _CK + h * HEAD_DIM:_CK + (h + 1) * HEAD_DIM] * K_SCALE
            v = z[j:j + 1, _CV + h * HEAD_DIM:_CV + (h + 1) * HEAD_DIM]
            ah = a[j:j + 1, h:h + 1]
            wh = w[j:j + 1, h:h + 1]
            c0 = c0_ref[j, h]
            q8 = jnp.broadcast_to(q, (8, HEAD_DIM)).astype(BF16)
            rows.append(_dot(q8, c0.astype(BF16))[0:1, :])
            kw8 = jnp.where(sub == 0, jnp.broadcast_to(k * wh, (8, HEAD_DIM)), 0.0).astype(BF16)
            v8 = jnp.broadcast_to(v, (8, HEAD_DIM)).astype(BF16)
            c1_ref[j, h] = ah * c0 + _dot_tn(kw8, v8)
        qc_ref[j:j + 1, :] = jnp.concatenate(rows, axis=1)


def _sample_state(z_s, m0, gate_bias_row, c0):
    n = z_s.shape[0]
    bt = STATE_TOKENS
    return pl.pallas_call(
        _sstate_kernel,
        grid=(n // bt,),
        in_specs=[
            pl.BlockSpec((bt, z_s.shape[1]), lambda i: (i, 0)),
            pl.BlockSpec((bt, N_HEADS), lambda i: (i, 0)),
            pl.BlockSpec((1, 2 * N_HEADS), lambda i: (0, 0)),
            pl.BlockSpec((bt, N_HEADS, HEAD_DIM, HEAD_DIM), lambda i: (i, 0, 0, 0)),
        ],
        out_specs=[
            pl.BlockSpec((bt, N_HEADS, HEAD_DIM, HEAD_DIM), lambda i: (i, 0, 0, 0)),
            pl.BlockSpec((bt, MLSTM_WIDTH), lambda i: (i, 0)),
        ],
        out_shape=[
            jax.ShapeDtypeStruct(c0.shape, F32),
            jax.ShapeDtypeStruct((n, MLSTM_WIDTH), F32),
        ],
        compiler_params=pltpu.CompilerParams(
            dimension_semantics=("arbitrary",), vmem_limit_bytes=VMEM_LIMIT),
        name="sample_state",
    )(z_s, m0, gate_bias_row, c0)


def _smix_kernel(x_ref, z_ref, qc_ref, mod_ref, n0_ref, m0_ref, pool_ref, gb_ref, ghead_ref,
                 wpool_ref, pscale_ref, wout_ref, gpost_ref,
                 x1_ref, n1_ref, m1_ref, poolo_ref):
    x = x_ref[...]
    z = z_ref[...]
    mod = mod_ref[...]
    ga1 = mod[:, 2 * D_MODEL:3 * D_MODEL]
    gates = z[:, _CG:_CG + 2 * N_HEADS] + gb_ref[...]
    m, w, a = _sample_gates(gates, m0_ref[...])
    m1_ref[...] = m
    ghead = ghead_ref[...]
    heads = []
    for h in range(N_HEADS):
        sl = slice(h * HEAD_DIM, (h + 1) * HEAD_DIM)
        q = z[:, _CQ + h * HEAD_DIM:_CQ + (h + 1) * HEAD_DIM]
        k = z[:, _CK + h * HEAD_DIM:_CK + (h + 1) * HEAD_DIM] * K_SCALE
        v = z[:, _CV + h * HEAD_DIM:_CV + (h + 1) * HEAD_DIM]
        o = z[:, _CO + h * HEAD_DIM:_CO + (h + 1) * HEAD_DIM]
        n0 = n0_ref[:, sl]
        ah = a[:, h:h + 1]
        wh = w[:, h:h + 1]
        s = jnp.sum(q * k, axis=-1, keepdims=True) * wh
        num = ah * qc_ref[:, sl] + s * v
        den = ah * jnp.sum(q * n0, axis=-1, keepdims=True) + s
        hh = num / jnp.maximum(jnp.abs(den), jnp.exp(-m[:, h:h + 1]))
        n1_ref[:, sl] = ah * n0 + wh * k
        hh = (_rms(hh) * ghead) * jax.nn.sigmoid(o)
        heads.append(hh.astype(BF16))

    u = z[:, _CU:_CU + POOL_WIDTH]
    g = POOL_GROUP_DIM
    pouts = []
    for gi, win in enumerate(POOL_WINDOWS):
        ug = u[:, gi * g:(gi + 1) * g]
        wsum = ug
        for r in range(POOL_BUF - (win - 1), POOL_BUF):
            wsum = wsum + pool_ref[r, :, gi * g:(gi + 1) * g]
        cnt = min(PAST_LEN + 1.0, float(win))
        pooled = wsum / cnt - ug
        pouts.append(_dot(pooled.astype(BF16), wpool_ref[gi].astype(BF16)))
    p_out = jnp.concatenate(pouts, axis=1) * pscale_ref[...]
    poolo_ref[0:POOL_BUF - 1] = pool_ref[1:POOL_BUF]
    poolo_ref[POOL_BUF - 1] = u

    mixh = jnp.concatenate(heads, axis=1)
    mix = (_dot(mixh, wout_ref[0:MLSTM_WIDTH, :].astype(BF16))
           + _dot(p_out.astype(BF16), wout_ref[MLSTM_WIDTH:, :].astype(BF16)))
    x1_ref[...] = x + ga1 * (_rms(mix) * gpost_ref[...])


def _sample_mix(x, z_s, qc, mod_s, n0, m0, pool_rows, gate_bias_row, g_head, w_pool, pool_scale, w_out, g_post1):
    n = x.shape[0]
    return pl.pallas_call(
        _smix_kernel,
        out_shape=[
            jax.ShapeDtypeStruct((n, D_MODEL), F32),
            jax.ShapeDtypeStruct((n, MLSTM_WIDTH), F32),
            jax.ShapeDtypeStruct((n, N_HEADS), F32),
            jax.ShapeDtypeStruct(pool_rows.shape, F32),
        ],
        compiler_params=pltpu.CompilerParams(vmem_limit_bytes=VMEM_LIMIT),
        name="sample_mix",
    )(x, z_s, qc, mod_s, n0, m0, pool_rows, gate_bias_row, g_head, w_pool, pool_scale, w_out, g_post1)


def kernel(x_prompt, x_sample, c_prompt, c_sample, state_C, state_n, state_m, state_pool, w_ada, b_ada,
           g_pre1, g_post1, w_in, b_ig, b_fg, g_head, w_pool, pool_scale, w_out, g_pre2, g_post2,
           w_up, w_down):
    depth = w_ada.shape[0]
    assert depth == 1, "single-layer step"
    bsz, seq, _ = x_prompt.shape
    nb = x_sample.shape[0]
    assert x_sample.shape[1] == 1 and seq % MIX_TOKENS == 0 and seq % MLP_ROWS == 0 and nb % STATE_TOKENS == 0
    assert w_in.shape[2] == IN_COLS
    l = 0

    gate_bias = jnp.concatenate([b_ig[l], b_fg[l]])
    gate_bias_col = jnp.pad(gate_bias, (0, GATE_ROWS - 2 * N_HEADS))[:, None]
    gate_bias_row = gate_bias[None, :]
    gpre1 = g_pre1[l][None, :]
    gpost1 = g_post1[l][None, :]
    gpre2 = g_pre2[l][None, :]
    gpost2 = g_post2[l][None, :]
    ghead = g_head[l][None, :]
    pscale = pool_scale[l][None, :]
    w_in_t = jnp.transpose(w_in[l])
    w_gate_t = w_in_t[_CG:_CG + 2 * N_HEADS]

    mod_p, mod_s = _ada(c_prompt, c_sample, w_ada[l], b_ada[l][None, :])
    mod_p = mod_p.reshape(bsz, 1, 6 * D_MODEL)

    x1p, c_p, n_p, m_p, pool_p = _mix_prompt(
        x_prompt, mod_p, gpre1, gpost1, w_in_t, w_gate_t, gate_bias_col, ghead, w_pool[l], pscale, w_out)

    xs = x_sample.reshape(nb, D_MODEL)
    m0 = state_m[l]
    z_s = _sample_proj(xs, mod_s, gpre1, w_in_t)
    c_s, qc = _sample_state(z_s, m0, gate_bias_row, state_C[l])
    x1s, n_s, m_s, pool_s = _sample_mix(
        xs, z_s, qc, mod_s, state_n[l].reshape(nb, MLSTM_WIDTH), m0,
        jnp.transpose(state_pool[l], (1, 0, 2)), gate_bias_row, ghead, w_pool[l], pscale,
        w_out[l], gpost1)

    y_p, y_s = _mlp(x1p.reshape(bsz * seq, D_MODEL), x1s, mod_p, mod_s, seq, gpre2, gpost2, w_up, w_down)

    return (y_p.reshape(bsz, seq, D_MODEL), y_s.reshape(nb, 1, D_MODEL),
            c_p[None], n_p.reshape(1, bsz, N_HEADS, HEAD_DIM), m_p.reshape(1, bsz, N_HEADS), pool_p[None],
            c_s[None], n_s.reshape(1, nb, N_HEADS, HEAD_DIM), m_s[None],
            jnp.transpose(pool_s, (1, 0, 2))[None])
```

```python
import functools

import jax
import jax.numpy as jnp
from jax import lax
from jax.experimental import pallas as pl
from jax.experimental.pallas import tpu as pltpu

F32 = jnp.float32
BF16 = jnp.bfloat16

D_MODEL = 1024
N_HEADS = 4
HEAD_DIM = 128
MLSTM_WIDTH = N_HEADS * HEAD_DIM
POOL_WIDTH = 512
POOL_WINDOWS = (2, 4, 8, 16)
POOL_GROUP_DIM = 128
POOL_BUF = 15
POOL_HIST = 16
D_FF = 4 * D_MODEL
EPS = 1e-6
PAST_LEN = 16384
K_SCALE = HEAD_DIM ** -0.5
GATE_ROWS = 16

_CQ, _CK, _CV, _CO = 0, 512, 1024, 1536
_CG = 2048
_CU = _CG + 2 * N_HEADS
IN_COLS = _CU + POOL_WIDTH

MIX_TOKENS = 1024
CHUNK = 256
MLP_ROWS = 1024
MLP_SUB = 512
FF_CHUNK = 1024
STATE_TOKENS = 16
ADA_COLS = 1024
STAGE_SLOTS = 3
VMEM_LIMIT = 60 * 1024 * 1024


def _dot(a, b):
    return jnp.dot(a, b, preferred_element_type=F32)


def _dot_nt(a, b):
    return lax.dot_general(a, b, (((1,), (1,)), ((), ())), preferred_element_type=F32)


def _dot_tn(a, b):
    return lax.dot_general(a, b, (((0,), (0,)), ((), ())), preferred_element_type=F32)


def _rms(x):
    return x * lax.rsqrt(jnp.mean(x * x, axis=-1, keepdims=True) + EPS)


def _stream_cast(srcs, stage, sem, sink):
    n_slots = stage.shape[0]
    ahead = n_slots - 1

    def copy(i):
        return pltpu.make_async_copy(srcs[i], stage.at[i % n_slots], sem.at[i % n_slots])

    for i in range(min(ahead, len(srcs))):
        copy(i).start()
    for i in range(len(srcs)):
        if i + ahead < len(srcs):
            copy(i + ahead).start()
        copy(i).wait()
        sink(i, stage[i % n_slots])


def _ada_kernel(cp_ref, cs_ref, w_ref, b_ref, op_ref, os_ref):
    c = jnp.concatenate([cp_ref[...], cs_ref[...]], axis=0)
    s = c * jax.nn.sigmoid(c)
    mod = _dot(s.astype(BF16), w_ref[...].astype(BF16)) + b_ref[...]
    n_p = cp_ref.shape[0]
    op_ref[...] = mod[0:n_p]
    os_ref[...] = mod[n_p:]


def _ada(c_p, c_s, w_ada, b_ada):
    n_p = c_p.shape[0]
    n_s = c_s.shape[0]
    tn = ADA_COLS
    return pl.pallas_call(
        _ada_kernel,
        grid=(6 * D_MODEL // tn,),
        in_specs=[
            pl.BlockSpec((n_p, D_MODEL), lambda j: (0, 0)),
            pl.BlockSpec((n_s, D_MODEL), lambda j: (0, 0)),
            pl.BlockSpec((D_MODEL, tn), lambda j: (0, j)),
            pl.BlockSpec((1, tn), lambda j: (0, j)),
        ],
        out_specs=[
            pl.BlockSpec((n_p, tn), lambda j: (0, j)),
            pl.BlockSpec((n_s, tn), lambda j: (0, j)),
        ],
        out_shape=[
            jax.ShapeDtypeStruct((n_p, 6 * D_MODEL), F32),
            jax.ShapeDtypeStruct((n_s, 6 * D_MODEL), F32),
        ],
        compiler_params=pltpu.CompilerParams(
            dimension_semantics=("arbitrary",), vmem_limit_bytes=VMEM_LIMIT),
        name="ada_mod",
    )(c_p, c_s, w_ada, b_ada)


def _cumsum_lanes(x, upper):
    hi = x.astype(BF16)
    r1 = x - hi.astype(F32)
    mid = r1.astype(BF16)
    lo = (r1 - mid.astype(F32)).astype(BF16)
    y = _dot(jnp.concatenate([hi, mid, lo], axis=0), upper)
    n = x.shape[0]
    return y[0:n] + y[n:2 * n] + y[2 * n:3 * n]


def _mix_kernel(x_ref, mod_ref, gpre_ref, gpost_ref, winT_hbm, big_ref, bfg_ref, ghead_ref,
                wpool_ref, pscale_ref, wout_hbm, wup_ref, wdn_ref,
                x1_ref, c_ref, n_ref, m_ref, pool_ref, wupb_ref, wdnb_ref,
                wtm_s, wkt_s, wgt_s, wout_s, stage, sem, cn_s, m_s, hist_s):
    tl = MIX_TOKENS
    lc = CHUNK
    n_chunks = tl // lc
    lane_tiles = lc // HEAD_DIM
    b = pl.program_id(0)
    t = pl.program_id(1)
    nt = pl.num_programs(1)

    @pl.when((b == 0) & (t == 0))
    def _():
        half = D_MODEL // 2
        srcs = [winT_hbm.at[pl.ds(c, MLSTM_WIDTH), :] for c in (_CQ, _CK, _CV, _CO, _CU)]
        srcs += [wout_hbm.at[0, pl.ds(r, half), :] for r in (0, half)]
        tm_col = {0: 0, 2: MLSTM_WIDTH, 3: 2 * MLSTM_WIDTH, 4: 3 * MLSTM_WIDTH}

        def sink(i, blk):
            if i == 1:
                wkt_s[...] = blk.astype(BF16)
            elif i in tm_col:
                wtm_s[:, tm_col[i]:tm_col[i] + MLSTM_WIDTH] = blk.T.astype(BF16)
            else:
                wout_s[(i - 5) * half:(i - 4) * half, :] = blk.astype(BF16)

        _stream_cast(srcs, stage, sem, sink)
        gate_dst = stage.at[0, pl.ds(0, 2 * N_HEADS), :]
        gate_copy = pltpu.make_async_copy(winT_hbm.at[pl.ds(_CG, 2 * N_HEADS), :], gate_dst, sem.at[0])
        gate_copy.start()
        gate_copy.wait()
        pad = jnp.zeros((GATE_ROWS - 2 * N_HEADS, D_MODEL), F32)
        wgt_s[...] = jnp.concatenate([stage[0, 0:2 * N_HEADS, :], pad], axis=0).astype(BF16)

    @pl.when(t == 0)
    def _():
        cn_s[...] = jnp.zeros_like(cn_s)
        m_s[...] = jnp.zeros_like(m_s)
        hist_s[...] = jnp.zeros_like(hist_s)

    wupb_ref[...] = wup_ref[...].astype(BF16)
    wdnb_ref[...] = wdn_ref[...].astype(BF16)

    gate_row = lax.broadcasted_iota(jnp.int32, (GATE_ROWS, 1), 0)
    gate_bias = jnp.zeros((GATE_ROWS, 1), F32)
    for h in range(N_HEADS):
        gate_bias = jnp.where(gate_row == h, big_ref[0, h], gate_bias)
        gate_bias = jnp.where(gate_row == N_HEADS + h, bfg_ref[0, h], gate_bias)

    mod = mod_ref[pl.ds(b, 1), :]
    sh1 = mod[:, 0:D_MODEL]
    sc1 = mod[:, D_MODEL:2 * D_MODEL]
    ga1 = mod[:, 2 * D_MODEL:3 * D_MODEL]
    gpre = gpre_ref[...]
    gpost = gpost_ref[...]
    ghead = ghead_ref[...]
    pscale = pscale_ref[...]
    row_i = lax.broadcasted_iota(jnp.int32, (lc, lc), 0)
    col_i = lax.broadcasted_iota(jnp.int32, (lc, lc), 1)
    causal = col_i <= row_i
    upper = (row_i <= col_i).astype(BF16)
    ones_blk = jnp.ones((lc, HEAD_DIM), BF16)
    g = POOL_GROUP_DIM

    def norm_and_gates(c):
        x = x_ref[c * lc:(c + 1) * lc, :]
        hnb = ((_rms(x) * gpre) * (1.0 + sc1) + sh1).astype(BF16)
        gt = _dot_nt(wgt_s[...], hnb) + gate_bias
        b16 = _cumsum_lanes(jax.nn.log_sigmoid(gt), upper)
        bcs = b16[N_HEADS:2 * N_HEADS]
        return hnb, gt[0:N_HEADS] - bcs, bcs, b16.T

    def project(hnb):
        ztm = _dot(hnb, wtm_s[...])
        kt_all = _dot_nt(wkt_s[...], hnb)
        return ztm, kt_all

    cns = [cn_s[h] for h in range(N_HEADS)]
    ms = [m_s[h] for h in range(N_HEADS)]
    hist = [hist_s[...]]
    last_ext = [None]

    def heads(ztm, kt_all, rr, bcs, bcs_col):
        outs = []
        for h in range(N_HEADS):
            sl = slice(h * HEAD_DIM, (h + 1) * HEAD_DIM)
            q = ztm[:, sl]
            v = ztm[:, MLSTM_WIDTH + h * HEAD_DIM:MLSTM_WIDTH + (h + 1) * HEAD_DIM]
            o = ztm[:, 2 * MLSTM_WIDTH + h * HEAD_DIM:2 * MLSTM_WIDTH + (h + 1) * HEAD_DIM]
            kt = kt_all[sl, :] * K_SCALE
            r_row = rr[h:h + 1, :]
            m0 = ms[h]
            cn = cns[h]

            rmat = jnp.where(causal, r_row, -jnp.inf)
            mcol = jnp.maximum(jnp.max(rmat, axis=-1, keepdims=True), m0)
            mcol_b = jnp.broadcast_to(mcol, (lc, HEAD_DIM))
            wmat = jnp.exp(rmat - jnp.concatenate([mcol_b] * lane_tiles, axis=1))
            a_b = jnp.exp(m0 - mcol_b)
            bcol = bcs_col[:, N_HEADS + h:N_HEADS + h + 1]

            qb = q.astype(BF16)
            vext = jnp.concatenate([v.astype(BF16), ones_blk], axis=1)
            s = _dot(qb, kt.astype(BF16)) * wmat
            qcn = _dot(qb, cn.astype(BF16))
            sv = _dot(s.astype(BF16), vext)
            nd = jnp.concatenate([a_b, a_b], axis=1) * qcn + sv
            num = nd[:, 0:HEAD_DIM]
            den = nd[:, HEAD_DIM:]
            hh = num / jnp.maximum(jnp.abs(den), jnp.exp(-(bcol + mcol)))
            hh = (_rms(hh) * ghead) * jax.nn.sigmoid(o)
            outs.append(hh.astype(BF16))

            ml = mcol[lc - 1:lc, :]
            al = jnp.exp(m0 - ml)
            wl = jnp.exp(r_row - ml)
            kw = (kt * wl).astype(BF16)
            cns[h] = al * cn + _dot(kw, vext)
            ms[h] = bcs[h:h + 1, lc - 1:lc] + ml
        return jnp.concatenate(outs, axis=1)

    def pool(c, u):
        ext = jnp.concatenate([hist[0], u], axis=0)
        p2 = ext + pltpu.roll(ext, 1, axis=0)
        p4 = p2[:, g:] + pltpu.roll(p2[:, g:], 2, axis=0)
        p8 = p4[:, g:] + pltpu.roll(p4[:, g:], 4, axis=0)
        p16 = p8[:, g:] + pltpu.roll(p8[:, g:], 8, axis=0)
        wsum = (p2[:, 0:g], p4[:, 0:g], p8[:, 0:g], p16)
        pos = (t * tl + c * lc + lax.broadcasted_iota(jnp.int32, (lc, 1), 0)).astype(F32)
        pouts = []
        for gi, win in enumerate(POOL_WINDOWS):
            cnt = jnp.minimum(pos + 1.0, float(win))
            ug = u[:, gi * g:(gi + 1) * g]
            pooled = wsum[gi][POOL_HIST:, :] / cnt - ug
            pouts.append(_dot(pooled.astype(BF16), wpool_ref[gi].astype(BF16)))
        hist[0] = ext[lc:, :]
        last_ext[0] = ext
        return (jnp.concatenate(pouts, axis=1) * pscale).astype(BF16)

    def finish(c, mixh, p_out):
        mix = _dot(mixh, wout_s[0:MLSTM_WIDTH, :]) + _dot(p_out, wout_s[MLSTM_WIDTH:, :])
        x = x_ref[c * lc:(c + 1) * lc, :]
        x1_ref[c * lc:(c + 1) * lc, :] = x + ga1 * (_rms(mix) * gpost)

    pres = []
    zs = []
    for c in range(n_chunks):
        pres.append(norm_and_gates(c))
        zs.append(project(pres[c][0]))
    mixhs = [heads(zs[c][0], zs[c][1], *pres[c][1:]) for c in range(n_chunks)]
    p_outs = [pool(c, zs[c][0][:, 3 * MLSTM_WIDTH:3 * MLSTM_WIDTH + POOL_WIDTH]) for c in range(n_chunks)]
    for c in range(n_chunks):
        finish(c, mixhs[c], p_outs[c])

    for h in range(N_HEADS):
        cn_s[h] = cns[h]
        m_s[h] = ms[h]
    hist_s[...] = hist[0]

    @pl.when(t == nt - 1)
    def _():
        for h in range(N_HEADS):
            c_ref[h] = cns[h][:, 0:HEAD_DIM]
            n_ref[h] = cns[h][:, HEAD_DIM:HEAD_DIM + 1]
            m_ref[h] = ms[h]
        pool_ref[...] = last_ext[0][lc + POOL_HIST - POOL_BUF:, :]


def _mix_prompt(x, mod_p, g_pre1, g_post1, w_in_t, b_ig, b_fg, g_head, w_pool, pool_scale, w_out, w_up, w_dn):
    bsz, seq, _ = x.shape
    tl = MIX_TOKENS
    nt = seq // tl
    steps = bsz * nt
    assert D_MODEL % (16 * steps) == 0
    up_rows = D_MODEL // steps
    dn_rows = D_FF // steps
    const2 = lambda b, t: (0, 0)
    const3 = lambda b, t: (0, 0, 0)
    share = lambda b, t: (0, b * nt + t, 0)
    smem = pl.BlockSpec(memory_space=pltpu.SMEM)
    return pl.pallas_call(
        _mix_kernel,
        grid=(bsz, nt),
        in_specs=[
            pl.BlockSpec((None, tl, D_MODEL), lambda b, t: (b, t, 0)),
            pl.BlockSpec(mod_p.shape, const2),
            pl.BlockSpec((1, D_MODEL), const2),
            pl.BlockSpec((1, D_MODEL), const2),
            pl.BlockSpec(memory_space=pl.ANY),
            smem,
            smem,
            pl.BlockSpec((1, HEAD_DIM), const2),
            pl.BlockSpec(w_pool.shape, const3),
            pl.BlockSpec((1, POOL_WIDTH), const2),
            pl.BlockSpec(memory_space=pl.ANY),
            pl.BlockSpec((None, up_rows, D_FF), share),
            pl.BlockSpec((None, dn_rows, D_MODEL), share),
        ],
        out_specs=[
            pl.BlockSpec((None, tl, D_MODEL), lambda b, t: (b, t, 0)),
            pl.BlockSpec((None, N_HEADS, HEAD_DIM, HEAD_DIM), lambda b, t: (b, 0, 0, 0)),
            pl.BlockSpec((None, N_HEADS, HEAD_DIM, 1), lambda b, t: (b, 0, 0, 0)),
            pl.BlockSpec((None, N_HEADS, 1, 1), lambda b, t: (b, 0, 0, 0)),
            pl.BlockSpec((None, POOL_BUF, POOL_WIDTH), lambda b, t: (b, 0, 0)),
            pl.BlockSpec((up_rows, D_FF), lambda b, t: (b * nt + t, 0)),
            pl.BlockSpec((dn_rows, D_MODEL), lambda b, t: (b * nt + t, 0)),
        ],
        out_shape=[
            jax.ShapeDtypeStruct((bsz, seq, D_MODEL), F32),
            jax.ShapeDtypeStruct((bsz, N_HEADS, HEAD_DIM, HEAD_DIM), F32),
            jax.ShapeDtypeStruct((bsz, N_HEADS, HEAD_DIM, 1), F32),
            jax.ShapeDtypeStruct((bsz, N_HEADS, 1, 1), F32),
            jax.ShapeDtypeStruct((bsz, POOL_BUF, POOL_WIDTH), F32),
            jax.ShapeDtypeStruct((D_MODEL, D_FF), BF16),
            jax.ShapeDtypeStruct((D_FF, D_MODEL), BF16),
        ],
        scratch_shapes=[
            pltpu.VMEM((D_MODEL, 3 * MLSTM_WIDTH + POOL_WIDTH), BF16),
            pltpu.VMEM((MLSTM_WIDTH, D_MODEL), BF16),
            pltpu.VMEM((GATE_ROWS, D_MODEL), BF16),
            pltpu.VMEM((D_MODEL, D_MODEL), BF16),
            pltpu.VMEM((STAGE_SLOTS, MLSTM_WIDTH, D_MODEL), F32),
            pltpu.SemaphoreType.DMA((STAGE_SLOTS,)),
            pltpu.VMEM((N_HEADS, HEAD_DIM, 2 * HEAD_DIM), F32),
            pltpu.VMEM((N_HEADS, 1, 1), F32),
            pltpu.VMEM((POOL_HIST, POOL_WIDTH), F32),
        ],
        compiler_params=pltpu.CompilerParams(
            dimension_semantics=("arbitrary", "arbitrary"), vmem_limit_bytes=VMEM_LIMIT),
        name="mix_prompt",
    )(x, mod_p, g_pre1, g_post1, w_in_t, b_ig, b_fg, g_head, w_pool, pool_scale, w_out, w_up, w_dn)


def _mlp_rows(x_ref, y_ref, mod, gpre, gpost, wup_s, wdn_s, sub):
    sh2 = mod[:, 3 * D_MODEL:4 * D_MODEL]
    sc2 = mod[:, 4 * D_MODEL:5 * D_MODEL]
    ga2 = mod[:, 5 * D_MODEL:6 * D_MODEL]
    n_sub = x_ref.shape[0] // sub
    per_row = mod.shape[0] > 1

    def rows(r, a):
        return a[r * sub:(r + 1) * sub] if per_row else a

    hbs = []
    for r in range(n_sub):
        x = x_ref[r * sub:(r + 1) * sub, :]
        hbs.append(((_rms(x) * gpre) * (1.0 + rows(r, sc2)) + rows(r, sh2)).astype(BF16))
    accs = [jnp.zeros((sub, D_MODEL), F32) for _ in range(n_sub)]
    n_ff = D_FF // FF_CHUNK
    for j in range(n_ff):
        for r in range(n_sub):
            f = _dot(hbs[r], wup_s[:, j * FF_CHUNK:(j + 1) * FF_CHUNK])
            f = jnp.square(jnp.maximum(f, 0.0))
            accs[r] = accs[r] + _dot(f.astype(BF16), wdn_s[j * FF_CHUNK:(j + 1) * FF_CHUNK, :])
            if j == n_ff - 1:
                x = x_ref[r * sub:(r + 1) * sub, :]
                y_ref[r * sub:(r + 1) * sub, :] = x + rows(r, ga2) * (_rms(accs[r]) * gpost)


def _mlp_kernel(xp_ref, xs_ref, modp_ref, mods_ref, gpre_ref, gpost_ref, wup_ref, wdn_ref,
                yp_ref, ys_ref, *, steps_per_mod):
    i = pl.program_id(0)
    n_prompt = pl.num_programs(0) - 1

    @pl.when(i < n_prompt)
    def _():
        mod = modp_ref[pl.ds(i // steps_per_mod, 1), :]
        _mlp_rows(xp_ref, yp_ref, mod, gpre_ref[...], gpost_ref[...], wup_ref, wdn_ref, MLP_SUB)

    @pl.when(i == n_prompt)
    def _():
        _mlp_rows(xs_ref, ys_ref, mods_ref[...], gpre_ref[...], gpost_ref[...], wup_ref, wdn_ref, xs_ref.shape[0])


def _mlp(xp, xs, mod_p, mod_s, rows_per_mod, g_pre2, g_post2, w_up_b, w_dn_b):
    n_p = xp.shape[0]
    n_s = xs.shape[0]
    tm = MLP_ROWS
    n_tiles = n_p // tm
    last = n_tiles - 1
    const2 = lambda i: (0, 0)
    ptile = lambda i: (jnp.minimum(i, last), 0)
    once = pl.Buffered(1)
    return pl.pallas_call(
        functools.partial(_mlp_kernel, steps_per_mod=rows_per_mod // tm),
        grid=(n_tiles + 1,),
        in_specs=[
            pl.BlockSpec((tm, D_MODEL), ptile),
            pl.BlockSpec((n_s, D_MODEL), const2, pipeline_mode=once),
            pl.BlockSpec(mod_p.shape, const2, pipeline_mode=once),
            pl.BlockSpec((n_s, 6 * D_MODEL), const2, pipeline_mode=once),
            pl.BlockSpec((1, D_MODEL), const2),
            pl.BlockSpec((1, D_MODEL), const2),
            pl.BlockSpec(w_up_b.shape, const2, pipeline_mode=once),
            pl.BlockSpec(w_dn_b.shape, const2, pipeline_mode=once),
        ],
        out_specs=[
            pl.BlockSpec((tm, D_MODEL), ptile),
            pl.BlockSpec((n_s, D_MODEL), const2),
        ],
        out_shape=[
            jax.ShapeDtypeStruct((n_p, D_MODEL), F32),
            jax.ShapeDtypeStruct((n_s, D_MODEL), F32),
        ],
        compiler_params=pltpu.CompilerParams(
            dimension_semantics=("arbitrary",), vmem_limit_bytes=VMEM_LIMIT),
        name="mlp",
    )(xp, xs, mod_p, mod_s, g_pre2, g_post2, w_up_b, w_dn_b)


def _sproj_kernel(x_ref, mod_ref, gpre_ref, w_ref, z_ref):
    x = x_ref[...]
    mod = mod_ref[...]
    sh1 = mod[:, 0:D_MODEL]
    sc1 = mod[:, D_MODEL:2 * D_MODEL]
    hn = (_rms(x) * gpre_ref[...]) * (1.0 + sc1) + sh1
    z_ref[...] = _dot_nt(hn.astype(BF16), w_ref[...].astype(BF16))


def _sample_proj(x, mod_s, g_pre1, w_in_t):
    n = x.shape[0]
    return pl.pallas_call(
        _sproj_kernel,
        out_shape=jax.ShapeDtypeStruct((n, IN_COLS), F32),
        compiler_params=pltpu.CompilerParams(vmem_limit_bytes=VMEM_LIMIT),
        name="sample_proj",
    )(x, mod_s, g_pre1, w_in_t)


def _sample_gates(gates, m0):
    ig = gates[:, 0:N_HEADS]
    logf = jax.nn.log_sigmoid(gates[:, N_HEADS:2 * N_HEADS])
    g = logf + m0
    m = jnp.maximum(g, ig)
    return m, jnp.exp(ig - m), jnp.exp(g - m)


def _sstate_kernel(z_ref, m0_ref, big_ref, bfg_ref, c0_ref, c1_ref, qc_ref):
    z = z_ref[...]
    gates = z[:, _CG:_CG + 2 * N_HEADS] + jnp.concatenate([big_ref[...], bfg_ref[...]], axis=1)
    _, w, a = _sample_gates(gates, m0_ref[...])
    sub = lax.broadcasted_iota(jnp.int32, (8, HEAD_DIM), 0)
    for j in range(STATE_TOKENS):
        rows = []
        for h in range(N_HEADS):
            q = z[j:j + 1, _CQ + h * HEAD_DIM:_CQ + (h + 1) * HEAD_DIM]
            k = z[j:j + 1, _CK + h * HEAD_DIM:_CK + (h + 1) * HEAD_DIM] * K_SCALE
            v = z[j:j + 1, _CV + h * HEAD_DIM:_CV + (h + 1) * HEAD_DIM]
            ah = a[j:j + 1, h:h + 1]
            wh = w[j:j + 1, h:h + 1]
            c0 = c0_ref[j, h]
            q8 = jnp.broadcast_to(q, (8, HEAD_DIM)).astype(BF16)
            rows.append(_dot(q8, c0.astype(BF16))[0:1, :])
            kw8 = jnp.where(sub == 0, jnp.broadcast_to(k * wh, (8, HEAD_DIM)), 0.0).astype(BF16)
            v8 = jnp.broadcast_to(v, (8, HEAD_DIM)).astype(BF16)
            c1_ref[j, h] = ah * c0 + _dot_tn(kw8, v8)
        qc_ref[j:j + 1, :] = jnp.concatenate(rows, axis=1)


def _sample_state(z_s, m0, b_ig, b_fg, c0):
    n = z_s.shape[0]
    bt = STATE_TOKENS
    return pl.pallas_call(
        _sstate_kernel,
        grid=(n // bt,),
        in_specs=[
            pl.BlockSpec((bt, z_s.shape[1]), lambda i: (i, 0)),
            pl.BlockSpec((bt, N_HEADS), lambda i: (i, 0)),
            pl.BlockSpec((1, N_HEADS), lambda i: (0, 0)),
            pl.BlockSpec((1, N_HEADS), lambda i: (0, 0)),
            pl.BlockSpec((bt, N_HEADS, HEAD_DIM, HEAD_DIM), lambda i: (i, 0, 0, 0)),
        ],
        out_specs=[
            pl.BlockSpec((bt, N_HEADS, HEAD_DIM, HEAD_DIM), lambda i: (i, 0, 0, 0)),
            pl.BlockSpec((bt, MLSTM_WIDTH), lambda i: (i, 0)),
        ],
        out_shape=[
            jax.ShapeDtypeStruct(c0.shape, F32),
            jax.ShapeDtypeStruct((n, MLSTM_WIDTH), F32),
        ],
        compiler_params=pltpu.CompilerParams(
            dimension_semantics=("arbitrary",), vmem_limit_bytes=VMEM_LIMIT),
        name="sample_state",
    )(z_s, m0, b_ig, b_fg, c0)


def _smix_kernel(x_ref, z_ref, qc_ref, mod_ref, n0_ref, m0_ref, pool_ref, big_ref, bfg_ref, ghead_ref,
                 wpool_ref, pscale_ref, wout_ref, gpost_ref,
                 x1_ref, n1_ref, m1_ref, poolo_ref):
    x = x_ref[...]
    z = z_ref[...]
    mod = mod_ref[...]
    ga1 = mod[:, 2 * D_MODEL:3 * D_MODEL]
    gates = z[:, _CG:_CG + 2 * N_HEADS] + jnp.concatenate([big_ref[...], bfg_ref[...]], axis=1)
    m, w, a = _sample_gates(gates, m0_ref[...])
    m1_ref[...] = m
    ghead = ghead_ref[...]
    heads = []
    for h in range(N_HEADS):
        sl = slice(h * HEAD_DIM, (h + 1) * HEAD_DIM)
        q = z[:, _CQ + h * HEAD_DIM:_CQ + (h + 1) * HEAD_DIM]
        k = z[:, _CK + h * HEAD_DIM:_CK + (h + 1) * HEAD_DIM] * K_SCALE
        v = z[:, _CV + h * HEAD_DIM:_CV + (h + 1) * HEAD_DIM]
        o = z[:, _CO + h * HEAD_DIM:_CO + (h + 1) * HEAD_DIM]
        n0 = n0_ref[:, sl]
        ah = a[:, h:h + 1]
        wh = w[:, h:h + 1]
        s = jnp.sum(q * k, axis=-1, keepdims=True) * wh
        num = ah * qc_ref[:, sl] + s * v
        den = ah * jnp.sum(q * n0, axis=-1, keepdims=True) + s
        hh = num / jnp.maximum(jnp.abs(den), jnp.exp(-m[:, h:h + 1]))
        n1_ref[:, sl] = ah * n0 + wh * k
        hh = (_rms(hh) * ghead) * jax.nn.sigmoid(o)
        heads.append(hh.astype(BF16))

    u = z[:, _CU:_CU + POOL_WIDTH]
    g = POOL_GROUP_DIM
    pouts = []
    for gi, win in enumerate(POOL_WINDOWS):
        ug = u[:, gi * g:(gi + 1) * g]
        wsum = ug
        for r in range(POOL_BUF - (win - 1), POOL_BUF):
            wsum = wsum + pool_ref[r, :, gi * g:(gi + 1) * g]
        cnt = min(PAST_LEN + 1.0, float(win))
        pooled = wsum / cnt - ug
        pouts.append(_dot(pooled.astype(BF16), wpool_ref[gi].astype(BF16)))
    p_out = jnp.concatenate(pouts, axis=1) * pscale_ref[...]
    poolo_ref[0:POOL_BUF - 1] = pool_ref[1:POOL_BUF]
    poolo_ref[POOL_BUF - 1] = u

    mixh = jnp.concatenate(heads, axis=1)
    mix = (_dot(mixh, wout_ref[0:MLSTM_WIDTH, :].astype(BF16))
           + _dot(p_out.astype(BF16), wout_ref[MLSTM_WIDTH:, :].astype(BF16)))
    x1_ref[...] = x + ga1 * (_rms(mix) * gpost_ref[...])


def _sample_mix(x, z_s, qc, mod_s, n0, m0, pool_rows, b_ig, b_fg, g_head, w_pool, pool_scale, w_out, g_post1):
    n = x.shape[0]
    return pl.pallas_call(
        _smix_kernel,
        out_shape=[
            jax.ShapeDtypeStruct((n, D_MODEL), F32),
            jax.ShapeDtypeStruct((n, MLSTM_WIDTH), F32),
            jax.ShapeDtypeStruct((n, N_HEADS), F32),
            jax.ShapeDtypeStruct(pool_rows.shape, F32),
        ],
        compiler_params=pltpu.CompilerParams(vmem_limit_bytes=VMEM_LIMIT),
        name="sample_mix",
    )(x, z_s, qc, mod_s, n0, m0, pool_rows, b_ig, b_fg, g_head, w_pool, pool_scale, w_out, g_post1)


def kernel(x_prompt, x_sample, c_prompt, c_sample, state_C, state_n, state_m, state_pool, w_ada, b_ada,
           g_pre1, g_post1, w_in, b_ig, b_fg, g_head, w_pool, pool_scale, w_out, g_pre2, g_post2,
           w_up, w_down):
    depth = w_ada.shape[0]
    assert depth == 1, "single-layer step"
    bsz, seq, _ = x_prompt.shape
    nb = x_sample.shape[0]
    assert x_sample.shape[1] == 1 and seq % MIX_TOKENS == 0 and seq % MLP_ROWS == 0 and nb % STATE_TOKENS == 0
    assert w_in.shape[2] == IN_COLS
    l = 0

    gpre1 = g_pre1[l][None, :]
    gpost1 = g_post1[l][None, :]
    gpre2 = g_pre2[l][None, :]
    gpost2 = g_post2[l][None, :]
    ghead = g_head[l][None, :]
    pscale = pool_scale[l][None, :]
    w_in_t = jnp.transpose(w_in[l])

    mod_p, mod_s = _ada(c_prompt, c_sample, w_ada[l], b_ada[l][None, :])

    x1p, c_p, n_p, m_p, pool_p, w_up_b, w_dn_b = _mix_prompt(
        x_prompt, mod_p, gpre1, gpost1, w_in_t, b_ig, b_fg, ghead, w_pool[l], pscale, w_out, w_up, w_down)

    xs = x_sample.reshape(nb, D_MODEL)
    m0 = state_m[l]
    z_s = _sample_proj(xs, mod_s, gpre1, w_in_t)
    c_s, qc = _sample_state(z_s, m0, b_ig, b_fg, state_C[l])
    x1s, n_s, m_s, pool_s = _sample_mix(
        xs, z_s, qc, mod_s, state_n[l].reshape(nb, MLSTM_WIDTH), m0,
        jnp.transpose(state_pool[l], (1, 0, 2)), b_ig, b_fg, ghead, w_pool[l], pscale,
        w_out[l], gpost1)

    y_p, y_s = _mlp(x1p.reshape(bsz * seq, D_MODEL), x1s, mod_p, mod_s, seq, gpre2, gpost2, w_up_b, w_dn_b)

    return (y_p.reshape(bsz, seq, D_MODEL), y_s.reshape(nb, 1, D_MODEL),
            c_p[None], n_p.reshape(1, bsz, N_HEADS, HEAD_DIM), m_p.reshape(1, bsz, N_HEADS), pool_p[None],
            c_s[None], n_s.reshape(1, nb, N_HEADS, HEAD_DIM), m_s[None],
            jnp.transpose(pool_s, (1, 0, 2))[None])
```

```python
import functools

import jax
import jax.numpy as jnp
from jax import lax
from jax.experimental import pallas as pl
from jax.experimental.pallas import tpu as pltpu

F32 = jnp.float32
BF16 = jnp.bfloat16

D_MODEL = 1024
N_HEADS = 4
HEAD_DIM = 128
MLSTM_WIDTH = N_HEADS * HEAD_DIM
POOL_WIDTH = 512
POOL_WINDOWS = (2, 4, 8, 16)
POOL_GROUP_DIM = 128
POOL_BUF = 15
POOL_HIST = 16
D_FF = 4 * D_MODEL
EPS = 1e-6
PAST_LEN = 16384
K_SCALE = HEAD_DIM ** -0.5
GATE_ROWS = 16

_CQ, _CK, _CV, _CO = 0, 512, 1024, 1536
_CG = 2048
_CU = _CG + 2 * N_HEADS
IN_COLS = _CU + POOL_WIDTH

MIX_TOKENS = 1024
CHUNK = 256
MLP_ROWS = 1024
MLP_SUB = 512
FF_CHUNK = 1024
STATE_TOKENS = 16
ADA_COLS = 1024
STAGE_SLOTS = 3
VMEM_LIMIT = 60 * 1024 * 1024


def _dot(a, b):
    return jnp.dot(a, b, preferred_element_type=F32)


def _dot_nt(a, b):
    return lax.dot_general(a, b, (((1,), (1,)), ((), ())), preferred_element_type=F32)


def _dot_tn(a, b):
    return lax.dot_general(a, b, (((0,), (0,)), ((), ())), preferred_element_type=F32)


def _rms(x):
    return x * lax.rsqrt(jnp.mean(x * x, axis=-1, keepdims=True) + EPS)


def _stream_cast(srcs, stage, sem, sink):
    n_slots = stage.shape[0]
    ahead = n_slots - 1

    def copy(i):
        return pltpu.make_async_copy(srcs[i], stage.at[i % n_slots], sem.at[i % n_slots])

    for i in range(min(ahead, len(srcs))):
        copy(i).start()
    for i in range(len(srcs)):
        if i + ahead < len(srcs):
            copy(i + ahead).start()
        copy(i).wait()
        sink(i, stage[i % n_slots])


def _ada_kernel(cp_ref, cs_ref, w_ref, b_ref, op_ref, os_ref):
    c = jnp.concatenate([cp_ref[...], cs_ref[...]], axis=0)
    s = c * jax.nn.sigmoid(c)
    mod = _dot(s.astype(BF16), w_ref[...].astype(BF16)) + b_ref[...]
    n_p = cp_ref.shape[0]
    op_ref[...] = mod[0:n_p]
    os_ref[...] = mod[n_p:]


def _ada(c_p, c_s, w_ada, b_ada):
    n_p = c_p.shape[0]
    n_s = c_s.shape[0]
    tn = ADA_COLS
    return pl.pallas_call(
        _ada_kernel,
        grid=(6 * D_MODEL // tn,),
        in_specs=[
            pl.BlockSpec((n_p, D_MODEL), lambda j: (0, 0)),
            pl.BlockSpec((n_s, D_MODEL), lambda j: (0, 0)),
            pl.BlockSpec((D_MODEL, tn), lambda j: (0, j)),
            pl.BlockSpec((1, tn), lambda j: (0, j)),
        ],
        out_specs=[
            pl.BlockSpec((n_p, tn), lambda j: (0, j)),
            pl.BlockSpec((n_s, tn), lambda j: (0, j)),
        ],
        out_shape=[
            jax.ShapeDtypeStruct((n_p, 6 * D_MODEL), F32),
            jax.ShapeDtypeStruct((n_s, 6 * D_MODEL), F32),
        ],
        compiler_params=pltpu.CompilerParams(
            dimension_semantics=("arbitrary",), vmem_limit_bytes=VMEM_LIMIT),
        name="ada_mod",
    )(c_p, c_s, w_ada, b_ada)


def _cumsum_lanes(x, upper):
    hi = x.astype(BF16)
    r1 = x - hi.astype(F32)
    mid = r1.astype(BF16)
    lo = (r1 - mid.astype(F32)).astype(BF16)
    y = _dot(jnp.concatenate([hi, mid, lo], axis=0), upper)
    n = x.shape[0]
    return y[0:n] + y[n:2 * n] + y[2 * n:3 * n]


def _mix_kernel(x_ref, mod_ref, gpre_ref, gpost_ref, winT_hbm, big_ref, bfg_ref, ghead_ref,
                wpool_ref, pscale_ref, wout_hbm, wup_ref, wdn_ref,
                x1_ref, c_ref, n_ref, m_ref, pool_ref, wupb_ref, wdnb_ref,
                wtm_s, wkt_s, wgt_s, wout_s, stage, sem, cn_s, m_s, hist_s):
    tl = MIX_TOKENS
    lc = CHUNK
    n_chunks = tl // lc
    lane_tiles = lc // HEAD_DIM
    b = pl.program_id(0)
    t = pl.program_id(1)
    nt = pl.num_programs(1)

    @pl.when((b == 0) & (t == 0))
    def _():
        half = D_MODEL // 2
        srcs = [winT_hbm.at[pl.ds(c, MLSTM_WIDTH), :] for c in (_CQ, _CK, _CV, _CO, _CU)]
        srcs += [wout_hbm.at[0, pl.ds(r, half), :] for r in (0, half)]
        tm_col = {0: 0, 2: MLSTM_WIDTH, 3: 2 * MLSTM_WIDTH, 4: 3 * MLSTM_WIDTH}

        def sink(i, blk):
            if i == 1:
                wkt_s[...] = blk.astype(BF16)
            elif i in tm_col:
                wtm_s[:, tm_col[i]:tm_col[i] + MLSTM_WIDTH] = blk.T.astype(BF16)
            else:
                wout_s[(i - 5) * half:(i - 4) * half, :] = blk.astype(BF16)

        _stream_cast(srcs, stage, sem, sink)
        gate_dst = stage.at[0, pl.ds(0, 2 * N_HEADS), :]
        gate_copy = pltpu.make_async_copy(winT_hbm.at[pl.ds(_CG, 2 * N_HEADS), :], gate_dst, sem.at[0])
        gate_copy.start()
        gate_copy.wait()
        pad = jnp.zeros((GATE_ROWS - 2 * N_HEADS, D_MODEL), F32)
        wgt_s[...] = jnp.concatenate([stage[0, 0:2 * N_HEADS, :], pad], axis=0).astype(BF16)
        for gi in range(len(POOL_WINDOWS)):
            cols = slice(gi * POOL_GROUP_DIM, (gi + 1) * POOL_GROUP_DIM)
            rows = slice(MLSTM_WIDTH + gi * POOL_GROUP_DIM, MLSTM_WIDTH + (gi + 1) * POOL_GROUP_DIM)
            wg = (wpool_ref[gi] * pscale_ref[:, cols]).astype(BF16)
            wout_s[rows, :] = _dot(wg, wout_s[rows, :]).astype(BF16)

    @pl.when(t == 0)
    def _():
        cn_s[...] = jnp.zeros_like(cn_s)
        m_s[...] = jnp.zeros_like(m_s)
        hist_s[...] = jnp.zeros_like(hist_s)

    wupb_ref[...] = wup_ref[...].astype(BF16)
    wdnb_ref[...] = wdn_ref[...].astype(BF16)

    gate_row = lax.broadcasted_iota(jnp.int32, (GATE_ROWS, 1), 0)
    gate_bias = jnp.zeros((GATE_ROWS, 1), F32)
    for h in range(N_HEADS):
        gate_bias = jnp.where(gate_row == h, big_ref[0, h], gate_bias)
        gate_bias = jnp.where(gate_row == N_HEADS + h, bfg_ref[0, h], gate_bias)

    mod = mod_ref[pl.ds(b, 1), :]
    sh1 = mod[:, 0:D_MODEL]
    sc1 = mod[:, D_MODEL:2 * D_MODEL]
    ga1 = mod[:, 2 * D_MODEL:3 * D_MODEL]
    gpre = gpre_ref[...]
    gpost = gpost_ref[...]
    ghead = ghead_ref[...]
    row_i = lax.broadcasted_iota(jnp.int32, (lc, lc), 0)
    col_i = lax.broadcasted_iota(jnp.int32, (lc, lc), 1)
    causal = col_i <= row_i
    upper = (row_i <= col_i).astype(BF16)
    ones_blk = jnp.ones((lc, HEAD_DIM), BF16)
    g = POOL_GROUP_DIM

    def norm_and_gates(c):
        x = x_ref[c * lc:(c + 1) * lc, :]
        hnb = ((_rms(x) * gpre) * (1.0 + sc1) + sh1).astype(BF16)
        gt = _dot_nt(wgt_s[...], hnb) + gate_bias
        b16 = _cumsum_lanes(jax.nn.log_sigmoid(gt), upper)
        bcs = b16[N_HEADS:2 * N_HEADS]
        return hnb, gt[0:N_HEADS] - bcs, bcs, b16.T

    def project(hnb):
        ztm = _dot(hnb, wtm_s[...])
        kt_all = _dot_nt(wkt_s[...], hnb)
        return ztm, kt_all

    cns = [cn_s[h] for h in range(N_HEADS)]
    ms = [m_s[h] for h in range(N_HEADS)]
    hist = [hist_s[...]]
    last_ext = [None]

    def heads(ztm, kt_all, rr, bcs, bcs_col):
        outs = []
        for h in range(N_HEADS):
            sl = slice(h * HEAD_DIM, (h + 1) * HEAD_DIM)
            q = ztm[:, sl]
            v = ztm[:, MLSTM_WIDTH + h * HEAD_DIM:MLSTM_WIDTH + (h + 1) * HEAD_DIM]
            o = ztm[:, 2 * MLSTM_WIDTH + h * HEAD_DIM:2 * MLSTM_WIDTH + (h + 1) * HEAD_DIM]
            kt = kt_all[sl, :] * K_SCALE
            r_row = rr[h:h + 1, :]
            m0 = ms[h]
            cn = cns[h]

            rmat = jnp.where(causal, r_row, -jnp.inf)
            mcol = jnp.maximum(jnp.max(rmat, axis=-1, keepdims=True), m0)
            mcol_b = jnp.broadcast_to(mcol, (lc, HEAD_DIM))
            wmat = jnp.exp(rmat - jnp.concatenate([mcol_b] * lane_tiles, axis=1))
            a_b = jnp.exp(m0 - mcol_b)
            bcol = bcs_col[:, N_HEADS + h:N_HEADS + h + 1]

            qb = q.astype(BF16)
            vext = jnp.concatenate([v.astype(BF16), ones_blk], axis=1)
            s = _dot(qb, kt.astype(BF16)) * wmat
            qcn = _dot(qb, cn.astype(BF16))
            sv = _dot(s.astype(BF16), vext)
            nd = jnp.concatenate([a_b, a_b], axis=1) * qcn + sv
            num = nd[:, 0:HEAD_DIM]
            den = nd[:, HEAD_DIM:]
            hh = num / jnp.maximum(jnp.abs(den), jnp.exp(-(bcol + mcol)))
            hh = (_rms(hh) * ghead) * jax.nn.sigmoid(o)
            outs.append(hh.astype(BF16))

            ml = mcol[lc - 1:lc, :]
            al = jnp.exp(m0 - ml)
            wl = jnp.exp(r_row - ml)
            kw = (kt * wl).astype(BF16)
            cns[h] = al * cn + _dot(kw, vext)
            ms[h] = bcs[h:h + 1, lc - 1:lc] + ml
        return jnp.concatenate(outs, axis=1)

    def pool(c, u):
        ext = jnp.concatenate([hist[0], u], axis=0)
        p2 = ext + pltpu.roll(ext, 1, axis=0)
        p4 = p2[:, g:] + pltpu.roll(p2[:, g:], 2, axis=0)
        p8 = p4[:, g:] + pltpu.roll(p4[:, g:], 4, axis=0)
        p16 = p8[:, g:] + pltpu.roll(p8[:, g:], 8, axis=0)
        wsum = (p2[:, 0:g], p4[:, 0:g], p8[:, 0:g], p16)
        pos = (t * tl + c * lc + lax.broadcasted_iota(jnp.int32, (lc, 1), 0)).astype(F32)
        pouts = []
        for gi, win in enumerate(POOL_WINDOWS):
            cnt = jnp.minimum(pos + 1.0, float(win))
            ug = u[:, gi * g:(gi + 1) * g]
            pooled = wsum[gi][POOL_HIST:, :] / cnt - ug
            pouts.append(pooled.astype(BF16))
        hist[0] = ext[lc:, :]
        last_ext[0] = ext
        return jnp.concatenate(pouts, axis=1)

    def finish(c, mixh, p_out):
        mix = _dot(mixh, wout_s[0:MLSTM_WIDTH, :]) + _dot(p_out, wout_s[MLSTM_WIDTH:, :])
        x = x_ref[c * lc:(c + 1) * lc, :]
        x1_ref[c * lc:(c + 1) * lc, :] = x + ga1 * (_rms(mix) * gpost)

    pres = []
    zs = []
    for c in range(n_chunks):
        pres.append(norm_and_gates(c))
        zs.append(project(pres[c][0]))
    mixhs = [heads(zs[c][0], zs[c][1], *pres[c][1:]) for c in range(n_chunks)]
    p_outs = [pool(c, zs[c][0][:, 3 * MLSTM_WIDTH:3 * MLSTM_WIDTH + POOL_WIDTH]) for c in range(n_chunks)]
    for c in range(n_chunks):
        finish(c, mixhs[c], p_outs[c])

    for h in range(N_HEADS):
        cn_s[h] = cns[h]
        m_s[h] = ms[h]
    hist_s[...] = hist[0]

    @pl.when(t == nt - 1)
    def _():
        for h in range(N_HEADS):
            c_ref[h] = cns[h][:, 0:HEAD_DIM]
            n_ref[h] = cns[h][:, HEAD_DIM:HEAD_DIM + 1]
            m_ref[h] = ms[h]
        pool_ref[...] = last_ext[0][lc + POOL_HIST - POOL_BUF:, :]


def _mix_prompt(x, mod_p, g_pre1, g_post1, w_in_t, b_ig, b_fg, g_head, w_pool, pool_scale, w_out, w_up, w_dn):
    bsz, seq, _ = x.shape
    tl = MIX_TOKENS
    nt = seq // tl
    steps = bsz * nt
    assert D_MODEL % (16 * steps) == 0
    up_rows = D_MODEL // steps
    dn_rows = D_FF // steps
    const2 = lambda b, t: (0, 0)
    const3 = lambda b, t: (0, 0, 0)
    share = lambda b, t: (0, b * nt + t, 0)
    smem = pl.BlockSpec(memory_space=pltpu.SMEM)
    return pl.pallas_call(
        _mix_kernel,
        grid=(bsz, nt),
        in_specs=[
            pl.BlockSpec((None, tl, D_MODEL), lambda b, t: (b, t, 0)),
            pl.BlockSpec(mod_p.shape, const2),
            pl.BlockSpec((1, D_MODEL), const2),
            pl.BlockSpec((1, D_MODEL), const2),
            pl.BlockSpec(memory_space=pl.ANY),
            smem,
            smem,
            pl.BlockSpec((1, HEAD_DIM), const2),
            pl.BlockSpec(w_pool.shape, const3),
            pl.BlockSpec((1, POOL_WIDTH), const2),
            pl.BlockSpec(memory_space=pl.ANY),
            pl.BlockSpec((None, up_rows, D_FF), share),
            pl.BlockSpec((None, dn_rows, D_MODEL), share),
        ],
        out_specs=[
            pl.BlockSpec((None, tl, D_MODEL), lambda b, t: (b, t, 0)),
            pl.BlockSpec((None, N_HEADS, HEAD_DIM, HEAD_DIM), lambda b, t: (b, 0, 0, 0)),
            pl.BlockSpec((None, N_HEADS, HEAD_DIM, 1), lambda b, t: (b, 0, 0, 0)),
            pl.BlockSpec((None, N_HEADS, 1, 1), lambda b, t: (b, 0, 0, 0)),
            pl.BlockSpec((None, POOL_BUF, POOL_WIDTH), lambda b, t: (b, 0, 0)),
            pl.BlockSpec((up_rows, D_FF), lambda b, t: (b * nt + t, 0)),
            pl.BlockSpec((dn_rows, D_MODEL), lambda b, t: (b * nt + t, 0)),
        ],
        out_shape=[
            jax.ShapeDtypeStruct((bsz, seq, D_MODEL), F32),
            jax.ShapeDtypeStruct((bsz, N_HEADS, HEAD_DIM, HEAD_DIM), F32),
            jax.ShapeDtypeStruct((bsz, N_HEADS, HEAD_DIM, 1), F32),
            jax.ShapeDtypeStruct((bsz, N_HEADS, 1, 1), F32),
            jax.ShapeDtypeStruct((bsz, POOL_BUF, POOL_WIDTH), F32),
            jax.ShapeDtypeStruct((D_MODEL, D_FF), BF16),
            jax.ShapeDtypeStruct((D_FF, D_MODEL), BF16),
        ],
        scratch_shapes=[
            pltpu.VMEM((D_MODEL, 3 * MLSTM_WIDTH + POOL_WIDTH), BF16),
            pltpu.VMEM((MLSTM_WIDTH, D_MODEL), BF16),
            pltpu.VMEM((GATE_ROWS, D_MODEL), BF16),
            pltpu.VMEM((D_MODEL, D_MODEL), BF16),
            pltpu.VMEM((STAGE_SLOTS, MLSTM_WIDTH, D_MODEL), F32),
            pltpu.SemaphoreType.DMA((STAGE_SLOTS,)),
            pltpu.VMEM((N_HEADS, HEAD_DIM, 2 * HEAD_DIM), F32),
            pltpu.VMEM((N_HEADS, 1, 1), F32),
            pltpu.VMEM((POOL_HIST, POOL_WIDTH), F32),
        ],
        compiler_params=pltpu.CompilerParams(
            dimension_semantics=("arbitrary", "arbitrary"), vmem_limit_bytes=VMEM_LIMIT),
        name="mix_prompt",
    )(x, mod_p, g_pre1, g_post1, w_in_t, b_ig, b_fg, g_head, w_pool, pool_scale, w_out, w_up, w_dn)


def _mlp_rows(x_ref, y_ref, mod, gpre, gpost, wup_s, wdn_s, sub):
    sh2 = mod[:, 3 * D_MODEL:4 * D_MODEL]
    sc2 = mod[:, 4 * D_MODEL:5 * D_MODEL]
    ga2 = mod[:, 5 * D_MODEL:6 * D_MODEL]
    n_sub = x_ref.shape[0] // sub
    per_row = mod.shape[0] > 1

    def rows(r, a):
        return a[r * sub:(r + 1) * sub] if per_row else a

    hbs = []
    for r in range(n_sub):
        x = x_ref[r * sub:(r + 1) * sub, :]
        hbs.append(((_rms(x) * gpre) * (1.0 + rows(r, sc2)) + rows(r, sh2)).astype(BF16))
    accs = [jnp.zeros((sub, D_MODEL), F32) for _ in range(n_sub)]
    n_ff = D_FF // FF_CHUNK
    for j in range(n_ff):
        for r in range(n_sub):
            f = _dot(hbs[r], wup_s[:, j * FF_CHUNK:(j + 1) * FF_CHUNK])
            f = jnp.square(jnp.maximum(f, 0.0))
            accs[r] = accs[r] + _dot(f.astype(BF16), wdn_s[j * FF_CHUNK:(j + 1) * FF_CHUNK, :])
            if j == n_ff - 1:
                x = x_ref[r * sub:(r + 1) * sub, :]
                y = x + rows(r, ga2) * (_rms(accs[r]) * gpost)
                if len(y_ref.shape) == 3:
                    y_ref[r * sub:(r + 1) * sub, 0, :] = y
                else:
                    y_ref[r * sub:(r + 1) * sub, :] = y


def _mlp_kernel(xp_ref, xs_ref, modp_ref, mods_ref, gpre_ref, gpost_ref, wup_ref, wdn_ref,
                yp_ref, ys_ref, *, steps_per_mod):
    i = pl.program_id(0)
    n_prompt = pl.num_programs(0) - 1

    @pl.when(i < n_prompt)
    def _():
        mod = modp_ref[pl.ds(i // steps_per_mod, 1), :]
        _mlp_rows(xp_ref, yp_ref, mod, gpre_ref[...], gpost_ref[...], wup_ref, wdn_ref, MLP_SUB)

    @pl.when(i == n_prompt)
    def _():
        _mlp_rows(xs_ref, ys_ref, mods_ref[...], gpre_ref[...], gpost_ref[...], wup_ref, wdn_ref, xs_ref.shape[0])


def _mlp(xp, xs, mod_p, mod_s, rows_per_mod, g_pre2, g_post2, w_up_b, w_dn_b):
    n_p = xp.shape[0]
    n_s = xs.shape[0]
    tm = MLP_ROWS
    n_tiles = n_p // tm
    last = n_tiles - 1
    const2 = lambda i: (0, 0)
    ptile = lambda i: (jnp.minimum(i, last), 0)
    once = pl.Buffered(1)
    return pl.pallas_call(
        functools.partial(_mlp_kernel, steps_per_mod=rows_per_mod // tm),
        grid=(n_tiles + 1,),
        in_specs=[
            pl.BlockSpec((tm, D_MODEL), ptile),
            pl.BlockSpec((n_s, D_MODEL), const2, pipeline_mode=once),
            pl.BlockSpec(mod_p.shape, const2, pipeline_mode=once),
            pl.BlockSpec((n_s, 6 * D_MODEL), const2, pipeline_mode=once),
            pl.BlockSpec((1, D_MODEL), const2),
            pl.BlockSpec((1, D_MODEL), const2),
            pl.BlockSpec(w_up_b.shape, const2, pipeline_mode=once),
            pl.BlockSpec(w_dn_b.shape, const2, pipeline_mode=once),
        ],
        out_specs=[
            pl.BlockSpec((tm, D_MODEL), ptile),
            pl.BlockSpec((n_s, 1, D_MODEL), lambda i: (0, 0, 0)),
        ],
        out_shape=[
            jax.ShapeDtypeStruct((n_p, D_MODEL), F32),
            jax.ShapeDtypeStruct((n_s, 1, D_MODEL), F32),
        ],
        compiler_params=pltpu.CompilerParams(
            dimension_semantics=("arbitrary",), vmem_limit_bytes=VMEM_LIMIT),
        name="mlp",
    )(xp, xs, mod_p, mod_s, g_pre2, g_post2, w_up_b, w_dn_b)


def _sproj_kernel(x_ref, mod_ref, gpre_ref, w_ref, z_ref):
    x = x_ref[:, 0, :]
    mod = mod_ref[...]
    sh1 = mod[:, 0:D_MODEL]
    sc1 = mod[:, D_MODEL:2 * D_MODEL]
    hn = (_rms(x) * gpre_ref[...]) * (1.0 + sc1) + sh1
    z_ref[...] = _dot_nt(hn.astype(BF16), w_ref[...].astype(BF16))


def _sample_proj(x, mod_s, g_pre1, w_in_t):
    n = x.shape[0]
    return pl.pallas_call(
        _sproj_kernel,
        out_shape=jax.ShapeDtypeStruct((n, IN_COLS), F32),
        compiler_params=pltpu.CompilerParams(vmem_limit_bytes=VMEM_LIMIT),
        name="sample_proj",
    )(x, mod_s, g_pre1, w_in_t)


def _sample_gates(gates, m0):
    ig = gates[:, 0:N_HEADS]
    logf = jax.nn.log_sigmoid(gates[:, N_HEADS:2 * N_HEADS])
    g = logf + m0
    m = jnp.maximum(g, ig)
    return m, jnp.exp(ig - m), jnp.exp(g - m)


def _sstate_kernel(z_ref, m0_ref, big_ref, bfg_ref, c0_ref, c1_ref, qc_ref):
    z = z_ref[...]
    gates = z[:, _CG:_CG + 2 * N_HEADS] + jnp.concatenate([big_ref[...], bfg_ref[...]], axis=1)
    _, w, a = _sample_gates(gates, m0_ref[...])
    sub = lax.broadcasted_iota(jnp.int32, (8, HEAD_DIM), 0)
    for j in range(STATE_TOKENS):
        rows = []
        for h in range(N_HEADS):
            q = z[j:j + 1, _CQ + h * HEAD_DIM:_CQ + (h + 1) * HEAD_DIM]
            k = z[j:j + 1, _CK + h * HEAD_DIM:_CK + (h + 1) * HEAD_DIM] * K_SCALE
            v = z[j:j + 1, _CV + h * HEAD_DIM:_CV + (h + 1) * HEAD_DIM]
            ah = a[j:j + 1, h:h + 1]
            wh = w[j:j + 1, h:h + 1]
            c0 = c0_ref[j, h]
            q8 = jnp.broadcast_to(q, (8, HEAD_DIM)).astype(BF16)
            rows.append(_dot(q8, c0.astype(BF16))[0:1, :])
            kw8 = jnp.where(sub == 0, jnp.broadcast_to(k * wh, (8, HEAD_DIM)), 0.0).astype(BF16)
            v8 = jnp.broadcast_to(v, (8, HEAD_DIM)).astype(BF16)
            c1_ref[j, h] = ah * c0 + _dot_tn(kw8, v8)
        qc_ref[j:j + 1, :] = jnp.concatenate(rows, axis=1)


def _sample_state(z_s, m0, b_ig, b_fg, c0):
    n = z_s.shape[0]
    bt = STATE_TOKENS
    return pl.pallas_call(
        _sstate_kernel,
        grid=(n // bt,),
        in_specs=[
            pl.BlockSpec((bt, z_s.shape[1]), lambda i: (i, 0)),
            pl.BlockSpec((bt, N_HEADS), lambda i: (i, 0)),
            pl.BlockSpec((1, N_HEADS), lambda i: (0, 0)),
            pl.BlockSpec((1, N_HEADS), lambda i: (0, 0)),
            pl.BlockSpec((bt, N_HEADS, HEAD_DIM, HEAD_DIM), lambda i: (i, 0, 0, 0)),
        ],
        out_specs=[
            pl.BlockSpec((bt, N_HEADS, HEAD_DIM, HEAD_DIM), lambda i: (i, 0, 0, 0)),
            pl.BlockSpec((bt, MLSTM_WIDTH), lambda i: (i, 0)),
        ],
        out_shape=[
            jax.ShapeDtypeStruct(c0.shape, F32),
            jax.ShapeDtypeStruct((n, MLSTM_WIDTH), F32),
        ],
        compiler_params=pltpu.CompilerParams(
            dimension_semantics=("arbitrary",), vmem_limit_bytes=VMEM_LIMIT),
        name="sample_state",
    )(z_s, m0, b_ig, b_fg, c0)


def _smix_kernel(x_ref, z_ref, qc_ref, mod_ref, n0_ref, m0_ref, pool_ref, big_ref, bfg_ref, ghead_ref,
                 wpool_ref, pscale_ref, wout_ref, gpost_ref,
                 x1_ref, n1_ref, m1_ref, poolo_ref):
    x = x_ref[:, 0, :]
    z = z_ref[...]
    mod = mod_ref[...]
    ga1 = mod[:, 2 * D_MODEL:3 * D_MODEL]
    gates = z[:, _CG:_CG + 2 * N_HEADS] + jnp.concatenate([big_ref[...], bfg_ref[...]], axis=1)
    m, w, a = _sample_gates(gates, m0_ref[...])
    m1_ref[...] = m
    ghead = ghead_ref[...]
    heads = []
    for h in range(N_HEADS):
        sl = slice(h * HEAD_DIM, (h + 1) * HEAD_DIM)
        q = z[:, _CQ + h * HEAD_DIM:_CQ + (h + 1) * HEAD_DIM]
        k = z[:, _CK + h * HEAD_DIM:_CK + (h + 1) * HEAD_DIM] * K_SCALE
        v = z[:, _CV + h * HEAD_DIM:_CV + (h + 1) * HEAD_DIM]
        o = z[:, _CO + h * HEAD_DIM:_CO + (h + 1) * HEAD_DIM]
        n0 = n0_ref[:, h, :]
        ah = a[:, h:h + 1]
        wh = w[:, h:h + 1]
        s = jnp.sum(q * k, axis=-1, keepdims=True) * wh
        num = ah * qc_ref[:, sl] + s * v
        den = ah * jnp.sum(q * n0, axis=-1, keepdims=True) + s
        hh = num / jnp.maximum(jnp.abs(den), jnp.exp(-m[:, h:h + 1]))
        n1_ref[:, h, :] = ah * n0 + wh * k
        hh = (_rms(hh) * ghead) * jax.nn.sigmoid(o)
        heads.append(hh.astype(BF16))

    u = z[:, _CU:_CU + POOL_WIDTH]
    g = POOL_GROUP_DIM
    pouts = []
    for gi, win in enumerate(POOL_WINDOWS):
        ug = u[:, gi * g:(gi + 1) * g]
        wsum = ug
        for r in range(POOL_BUF - (win - 1), POOL_BUF):
            wsum = wsum + pool_ref[r, :, gi * g:(gi + 1) * g]
        cnt = min(PAST_LEN + 1.0, float(win))
        pooled = wsum / cnt - ug
        pouts.append(_dot(pooled.astype(BF16), wpool_ref[gi].astype(BF16)))
    p_out = jnp.concatenate(pouts, axis=1) * pscale_ref[...]
    poolo_ref[0:POOL_BUF - 1] = pool_ref[1:POOL_BUF]
    poolo_ref[POOL_BUF - 1] = u

    mixh = jnp.concatenate(heads, axis=1)
    mix = (_dot(mixh, wout_ref[0:MLSTM_WIDTH, :].astype(BF16))
           + _dot(p_out.astype(BF16), wout_ref[MLSTM_WIDTH:, :].astype(BF16)))
    x1_ref[...] = x + ga1 * (_rms(mix) * gpost_ref[...])


def _sample_mix(x, z_s, qc, mod_s, n0, m0, pool_rows, b_ig, b_fg, g_head, w_pool, pool_scale, w_out, g_post1):
    n = x.shape[0]
    return pl.pallas_call(
        _smix_kernel,
        out_shape=[
            jax.ShapeDtypeStruct((n, D_MODEL), F32),
            jax.ShapeDtypeStruct((n, N_HEADS, HEAD_DIM), F32),
            jax.ShapeDtypeStruct((n, N_HEADS), F32),
            jax.ShapeDtypeStruct(pool_rows.shape, F32),
        ],
        compiler_params=pltpu.CompilerParams(vmem_limit_bytes=VMEM_LIMIT),
        name="sample_mix",
    )(x, z_s, qc, mod_s, n0, m0, pool_rows, b_ig, b_fg, g_head, w_pool, pool_scale, w_out, g_post1)


def kernel(x_prompt, x_sample, c_prompt, c_sample, state_C, state_n, state_m, state_pool, w_ada, b_ada,
           g_pre1, g_post1, w_in, b_ig, b_fg, g_head, w_pool, pool_scale, w_out, g_pre2, g_post2,
           w_up, w_down):
    depth = w_ada.shape[0]
    assert depth == 1, "single-layer step"
    bsz, seq, _ = x_prompt.shape
    nb = x_sample.shape[0]
    assert x_sample.shape[1] == 1 and seq % MIX_TOKENS == 0 and seq % MLP_ROWS == 0 and nb % STATE_TOKENS == 0
    assert w_in.shape[2] == IN_COLS
    l = 0

    gpre1 = g_pre1[l][None, :]
    gpost1 = g_post1[l][None, :]
    gpre2 = g_pre2[l][None, :]
    gpost2 = g_post2[l][None, :]
    ghead = g_head[l][None, :]
    pscale = pool_scale[l][None, :]
    w_in_t = jnp.transpose(w_in[l])

    mod_p, mod_s = _ada(c_prompt, c_sample, w_ada[l], b_ada[l][None, :])

    x1p, c_p, n_p, m_p, pool_p, w_up_b, w_dn_b = _mix_prompt(
        x_prompt, mod_p, gpre1, gpost1, w_in_t, b_ig, b_fg, ghead, w_pool[l], pscale, w_out, w_up, w_down)

    m0 = state_m[l]
    z_s = _sample_proj(x_sample, mod_s, gpre1, w_in_t)
    c_s, qc = _sample_state(z_s, m0, b_ig, b_fg, state_C[l])
    x1s, n_s, m_s, pool_s = _sample_mix(
        x_sample, z_s, qc, mod_s, state_n[l], m0,
        jnp.transpose(state_pool[l], (1, 0, 2)), b_ig, b_fg, ghead, w_pool[l], pscale,
        w_out[l], gpost1)

    y_p, y_s = _mlp(x1p.reshape(bsz * seq, D_MODEL), x1s, mod_p, mod_s, seq, gpre2, gpost2, w_up_b, w_dn_b)

    return (y_p.reshape(bsz, seq, D_MODEL), y_s,
            c_p[None], n_p.reshape(1, bsz, N_HEADS, HEAD_DIM), m_p.reshape(1, bsz, N_HEADS), pool_p[None],
            c_s[None], n_s[None], m_s[None],
            jnp.transpose(pool_s, (1, 0, 2))[None])
```

```python
import functools

import jax
import jax.numpy as jnp
from jax import lax
from jax.experimental import pallas as pl
from jax.experimental.pallas import tpu as pltpu

F32 = jnp.float32
BF16 = jnp.bfloat16

D_MODEL = 1024
N_HEADS = 4
HEAD_DIM = 128
MLSTM_WIDTH = N_HEADS * HEAD_DIM
POOL_WIDTH = 512
POOL_WINDOWS = (2, 4, 8, 16)
POOL_GROUP_DIM = 128
POOL_BUF = 15
POOL_HIST = 16
D_FF = 4 * D_MODEL
EPS = 1e-6
PAST_LEN = 16384
K_SCALE = HEAD_DIM ** -0.5
GATE_ROWS = 16

_CQ, _CK, _CV, _CO = 0, 512, 1024, 1536
_CG = 2048
_CU = _CG + 2 * N_HEADS
IN_COLS = _CU + POOL_WIDTH

MIX_TOKENS = 1024
CHUNK = 256
MLP_ROWS = 1024
MLP_SUB = 512
FF_CHUNK = 1024
ADA_COLS = 1024
VMEM_LIMIT = 60 * 1024 * 1024


def _dot(a, b):
    return jnp.dot(a, b, preferred_element_type=F32)


def _dot_nt(a, b):
    return lax.dot_general(a, b, (((1,), (1,)), ((), ())), preferred_element_type=F32)


def _rms(x):
    return x * lax.rsqrt(jnp.mean(x * x, axis=-1, keepdims=True) + EPS)


def _stream_cast(srcs, slot, n_slots, sem, sink):
    ahead = n_slots - 1

    def copy(i):
        return pltpu.make_async_copy(srcs[i], slot(i % n_slots), sem.at[i % n_slots])

    for i in range(min(ahead, len(srcs))):
        copy(i).start()
    for i in range(len(srcs)):
        if i + ahead < len(srcs):
            copy(i + ahead).start()
        copy(i).wait()
        sink(i, slot(i % n_slots)[...])


def _ada_kernel(cp_ref, cs_ref, w_ref, b_ref, op_ref, os_ref):
    c = jnp.concatenate([cp_ref[...], cs_ref[...]], axis=0)
    s = c * jax.nn.sigmoid(c)
    mod = _dot(s.astype(BF16), w_ref[...].astype(BF16)) + b_ref[...]
    n_p = cp_ref.shape[0]
    op_ref[...] = mod[0:n_p]
    os_ref[...] = mod[n_p:]


def _ada(c_p, c_s, w_ada, b_ada):
    n_p = c_p.shape[0]
    n_s = c_s.shape[0]
    tn = ADA_COLS
    return pl.pallas_call(
        _ada_kernel,
        grid=(6 * D_MODEL // tn,),
        in_specs=[
            pl.BlockSpec((n_p, D_MODEL), lambda j: (0, 0)),
            pl.BlockSpec((n_s, D_MODEL), lambda j: (0, 0)),
            pl.BlockSpec((D_MODEL, tn), lambda j: (0, j)),
            pl.BlockSpec((1, tn), lambda j: (0, j)),
        ],
        out_specs=[
            pl.BlockSpec((n_p, tn), lambda j: (0, j)),
            pl.BlockSpec((n_s, tn), lambda j: (0, j)),
        ],
        out_shape=[
            jax.ShapeDtypeStruct((n_p, 6 * D_MODEL), F32),
            jax.ShapeDtypeStruct((n_s, 6 * D_MODEL), F32),
        ],
        compiler_params=pltpu.CompilerParams(
            dimension_semantics=("arbitrary",), vmem_limit_bytes=VMEM_LIMIT),
        name="ada_mod",
    )(c_p, c_s, w_ada, b_ada)


def _cumsum_lanes(x, upper):
    hi = x.astype(BF16)
    r1 = x - hi.astype(F32)
    mid = r1.astype(BF16)
    lo = (r1 - mid.astype(F32)).astype(BF16)
    y = _dot(jnp.concatenate([hi, mid, lo], axis=0), upper)
    n = x.shape[0]
    return y[0:n] + y[n:2 * n] + y[2 * n:3 * n]


def _mix_kernel(x_ref, mod_ref, gpre_ref, gpost_ref, winT_hbm, big_ref, bfg_ref, ghead_ref,
                wpool_ref, pscale_ref, wout_hbm, wup_ref, wdn_ref, zs_ref, sm0_ref, sbig_ref, sbfg_ref, sc0_ref,
                x1_ref, c_ref, n_ref, m_ref, pool_ref, wupb_ref, wdnb_ref, sc1_ref, sqc_ref,
                wtm_s, wkt_s, wgt_s, wout_s, sem, cn_s, m_s, hist_s):
    tl = MIX_TOKENS
    lc = CHUNK
    n_chunks = tl // lc
    lane_tiles = lc // HEAD_DIM
    b = pl.program_id(0)
    t = pl.program_id(1)
    nt = pl.num_programs(1)

    @pl.when((b == 0) & (t == 0))
    def _():
        half = D_MODEL // 2
        n_slots = tl // half

        def slot(k):
            return x1_ref.at[pl.ds(k * half, half), :]

        srcs = [winT_hbm.at[pl.ds(c, MLSTM_WIDTH), :] for c in (_CQ, _CK, _CV, _CO, _CU)]
        srcs += [wout_hbm.at[0, pl.ds(r, half), :] for r in (0, half)]
        tm_col = {0: 0, 2: MLSTM_WIDTH, 3: 2 * MLSTM_WIDTH, 4: 3 * MLSTM_WIDTH}

        def sink(i, blk):
            if i == 1:
                wkt_s[...] = blk.astype(BF16)
            elif i in tm_col:
                wtm_s[:, tm_col[i]:tm_col[i] + MLSTM_WIDTH] = blk.T.astype(BF16)
            else:
                wout_s[(i - 5) * half:(i - 4) * half, :] = blk.astype(BF16)

        _stream_cast(srcs, slot, n_slots, sem, sink)
        gate_dst = x1_ref.at[pl.ds(0, 2 * N_HEADS), :]
        gate_copy = pltpu.make_async_copy(winT_hbm.at[pl.ds(_CG, 2 * N_HEADS), :], gate_dst, sem.at[0])
        gate_copy.start()
        gate_copy.wait()
        pad = jnp.zeros((GATE_ROWS - 2 * N_HEADS, D_MODEL), F32)
        wgt_s[...] = jnp.concatenate([x1_ref[0:2 * N_HEADS, :], pad], axis=0).astype(BF16)
        for gi in range(len(POOL_WINDOWS)):
            cols = slice(gi * POOL_GROUP_DIM, (gi + 1) * POOL_GROUP_DIM)
            rows = slice(MLSTM_WIDTH + gi * POOL_GROUP_DIM, MLSTM_WIDTH + (gi + 1) * POOL_GROUP_DIM)
            wg = (wpool_ref[gi] * pscale_ref[:, cols]).astype(BF16)
            wout_s[rows, :] = _dot(wg, wout_s[rows, :]).astype(BF16)

    @pl.when(t == 0)
    def _():
        cn_s[...] = jnp.zeros_like(cn_s)
        m_s[...] = jnp.zeros_like(m_s)
        hist_s[...] = jnp.zeros_like(hist_s)

    wupb_ref[...] = wup_ref[...].astype(BF16)
    wdnb_ref[...] = wdn_ref[...].astype(BF16)

    gate_row = lax.broadcasted_iota(jnp.int32, (GATE_ROWS, 1), 0)
    gate_bias = jnp.zeros((GATE_ROWS, 1), F32)
    for h in range(N_HEADS):
        gate_bias = jnp.where(gate_row == h, big_ref[0, h], gate_bias)
        gate_bias = jnp.where(gate_row == N_HEADS + h, bfg_ref[0, h], gate_bias)

    mod = mod_ref[pl.ds(b, 1), :]
    sh1 = mod[:, 0:D_MODEL]
    sc1 = mod[:, D_MODEL:2 * D_MODEL]
    ga1 = mod[:, 2 * D_MODEL:3 * D_MODEL]
    gpre = gpre_ref[...]
    gpost = gpost_ref[...]
    ghead = ghead_ref[...]
    row_i = lax.broadcasted_iota(jnp.int32, (lc, lc), 0)
    col_i = lax.broadcasted_iota(jnp.int32, (lc, lc), 1)
    causal = col_i <= row_i
    upper = (row_i <= col_i).astype(BF16)
    ones_blk = jnp.ones((lc, HEAD_DIM), BF16)
    g = POOL_GROUP_DIM

    def norm_and_gates(c):
        x = x_ref[c * lc:(c + 1) * lc, :]
        hnb = ((_rms(x) * gpre) * (1.0 + sc1) + sh1).astype(BF16)
        gt = _dot_nt(wgt_s[...], hnb) + gate_bias
        b16 = _cumsum_lanes(jax.nn.log_sigmoid(gt), upper)
        bcs = b16[N_HEADS:2 * N_HEADS]
        return hnb, gt[0:N_HEADS] - bcs, bcs, b16.T

    def project(hnb):
        ztm = _dot(hnb, wtm_s[...])
        kt_all = _dot_nt(wkt_s[...], hnb)
        return ztm, kt_all

    cns = [cn_s[h] for h in range(N_HEADS)]
    ms = [m_s[h] for h in range(N_HEADS)]
    hist = [hist_s[...]]
    last_ext = [None]

    def heads(ztm, kt_all, rr, bcs, bcs_col):
        outs = []
        for h in range(N_HEADS):
            sl = slice(h * HEAD_DIM, (h + 1) * HEAD_DIM)
            q = ztm[:, sl]
            v = ztm[:, MLSTM_WIDTH + h * HEAD_DIM:MLSTM_WIDTH + (h + 1) * HEAD_DIM]
            o = ztm[:, 2 * MLSTM_WIDTH + h * HEAD_DIM:2 * MLSTM_WIDTH + (h + 1) * HEAD_DIM]
            kt = kt_all[sl, :] * K_SCALE
            r_row = rr[h:h + 1, :]
            m0 = ms[h]
            cn = cns[h]

            rmat = jnp.where(causal, r_row, -jnp.inf)
            mcol = jnp.maximum(jnp.max(rmat, axis=-1, keepdims=True), m0)
            mcol_b = jnp.broadcast_to(mcol, (lc, HEAD_DIM))
            wmat = jnp.exp(rmat - jnp.concatenate([mcol_b] * lane_tiles, axis=1))
            a_b = jnp.exp(m0 - mcol_b)
            bcol = bcs_col[:, N_HEADS + h:N_HEADS + h + 1]

            qb = q.astype(BF16)
            vext = jnp.concatenate([v.astype(BF16), ones_blk], axis=1)
            s = _dot(qb, kt.astype(BF16)) * wmat
            qcn = _dot(qb, cn.astype(BF16))
            sv = _dot(s.astype(BF16), vext)
            nd = jnp.concatenate([a_b, a_b], axis=1) * qcn + sv
            num = nd[:, 0:HEAD_DIM]
            den = nd[:, HEAD_DIM:]
            hh = num / jnp.maximum(jnp.abs(den), jnp.exp(-(bcol + mcol)))
            hh = (_rms(hh) * ghead) * jax.nn.sigmoid(o)
            outs.append(hh.astype(BF16))

            ml = mcol[lc - 1:lc, :]
            al = jnp.exp(m0 - ml)
            wl = jnp.exp(r_row - ml)
            kw = (kt * wl).astype(BF16)
            cns[h] = al * cn + _dot(kw, vext)
            ms[h] = bcs[h:h + 1, lc - 1:lc] + ml
        return jnp.concatenate(outs, axis=1)

    def pool(c, u):
        ext = jnp.concatenate([hist[0], u], axis=0)
        p2 = ext + pltpu.roll(ext, 1, axis=0)
        p4 = p2[:, g:] + pltpu.roll(p2[:, g:], 2, axis=0)
        p8 = p4[:, g:] + pltpu.roll(p4[:, g:], 4, axis=0)
        p16 = p8[:, g:] + pltpu.roll(p8[:, g:], 8, axis=0)
        wsum = (p2[:, 0:g], p4[:, 0:g], p8[:, 0:g], p16)
        pos = (t * tl + c * lc + lax.broadcasted_iota(jnp.int32, (lc, 1), 0)).astype(F32)
        pouts = []
        for gi, win in enumerate(POOL_WINDOWS):
            cnt = jnp.minimum(pos + 1.0, float(win))
            ug = u[:, gi * g:(gi + 1) * g]
            pooled = wsum[gi][POOL_HIST:, :] / cnt - ug
            pouts.append(pooled.astype(BF16))
        hist[0] = ext[lc:, :]
        last_ext[0] = ext
        return jnp.concatenate(pouts, axis=1)

    def finish(c, mixh, p_out):
        mix = _dot(mixh, wout_s[0:MLSTM_WIDTH, :]) + _dot(p_out, wout_s[MLSTM_WIDTH:, :])
        x = x_ref[c * lc:(c + 1) * lc, :]
        x1_ref[c * lc:(c + 1) * lc, :] = x + ga1 * (_rms(mix) * gpost)

    pres = []
    zs = []
    z_smp = zs_ref[...]
    bt = z_smp.shape[0]
    smp_gates = z_smp[:, _CG:_CG + 2 * N_HEADS] + jnp.concatenate([sbig_ref[...], sbfg_ref[...]], axis=1)
    _, smp_w, smp_a = _sample_gates(smp_gates, sm0_ref[...])

    def sample_memory_update(h):
        sl = slice(h * HEAD_DIM, (h + 1) * HEAD_DIM)
        q_t = z_smp[:, _CQ + h * HEAD_DIM:_CQ + (h + 1) * HEAD_DIM].T
        kw_t = (z_smp[:, _CK + h * HEAD_DIM:_CK + (h + 1) * HEAD_DIM] * K_SCALE * smp_w[:, h:h + 1]).T
        rows = []
        for j in range(bt):
            c0 = sc0_ref[j, h]
            v = z_smp[j:j + 1, _CV + h * HEAD_DIM:_CV + (h + 1) * HEAD_DIM]
            sc1_ref[j, h] = smp_a[j:j + 1, h:h + 1] * c0 + kw_t[:, j:j + 1] * v
            rows.append(jnp.sum(q_t[:, j:j + 1] * c0, axis=0, keepdims=True))
        sqc_ref[:, sl] = jnp.concatenate(rows, axis=0)

    for c in range(n_chunks):
        pres.append(norm_and_gates(c))
        zs.append(project(pres[c][0]))
        for h in range(c, N_HEADS, n_chunks):
            sample_memory_update(h)
    mixhs = [heads(zs[c][0], zs[c][1], *pres[c][1:]) for c in range(n_chunks)]
    p_outs = [pool(c, zs[c][0][:, 3 * MLSTM_WIDTH:3 * MLSTM_WIDTH + POOL_WIDTH]) for c in range(n_chunks)]
    for c in range(n_chunks):
        finish(c, mixhs[c], p_outs[c])

    for h in range(N_HEADS):
        cn_s[h] = cns[h]
        m_s[h] = ms[h]
    hist_s[...] = hist[0]

    @pl.when(t == nt - 1)
    def _():
        for h in range(N_HEADS):
            c_ref[h] = cns[h][:, 0:HEAD_DIM]
            n_ref[h] = cns[h][:, HEAD_DIM:HEAD_DIM + 1]
            m_ref[h] = ms[h]
        pool_ref[...] = last_ext[0][lc + POOL_HIST - POOL_BUF:, :]


def _mix_prompt(x, mod_p, g_pre1, g_post1, w_in_t, b_ig, b_fg, g_head, w_pool, pool_scale, w_out, w_up, w_dn,
                z_s, m0_s, c0_s):
    bsz, seq, _ = x.shape
    n_s = z_s.shape[0]
    tl = MIX_TOKENS
    nt = seq // tl
    steps = bsz * nt
    half = D_MODEL // 2
    assert D_MODEL % (16 * steps) == 0
    assert n_s % (8 * steps) == 0
    assert tl % half == 0
    up_rows = D_MODEL // steps
    dn_rows = D_FF // steps
    bt = n_s // steps
    const2 = lambda b, t: (0, 0)
    const3 = lambda b, t: (0, 0, 0)
    share = lambda b, t: (0, b * nt + t, 0)
    smem = pl.BlockSpec(memory_space=pltpu.SMEM)
    return pl.pallas_call(
        _mix_kernel,
        grid=(bsz, nt),
        in_specs=[
            pl.BlockSpec((None, tl, D_MODEL), lambda b, t: (b, t, 0)),
            pl.BlockSpec(mod_p.shape, const2),
            pl.BlockSpec((1, D_MODEL), const2),
            pl.BlockSpec((1, D_MODEL), const2),
            pl.BlockSpec(memory_space=pl.ANY),
            smem,
            smem,
            pl.BlockSpec((1, HEAD_DIM), const2),
            pl.BlockSpec(w_pool.shape, const3),
            pl.BlockSpec((1, POOL_WIDTH), const2),
            pl.BlockSpec(memory_space=pl.ANY),
            pl.BlockSpec((None, up_rows, D_FF), share),
            pl.BlockSpec((None, dn_rows, D_MODEL), share),
            pl.BlockSpec((bt, z_s.shape[1]), lambda b, t: (b * nt + t, 0)),
            pl.BlockSpec((bt, N_HEADS), lambda b, t: (b * nt + t, 0)),
            pl.BlockSpec((1, N_HEADS), const2),
            pl.BlockSpec((1, N_HEADS), const2),
            pl.BlockSpec((bt, N_HEADS, HEAD_DIM, HEAD_DIM), lambda b, t: (b * nt + t, 0, 0, 0)),
        ],
        out_specs=[
            pl.BlockSpec((tl, D_MODEL), lambda b, t: (b * nt + t, 0)),
            pl.BlockSpec((None, N_HEADS, HEAD_DIM, HEAD_DIM), lambda b, t: (b, 0, 0, 0)),
            pl.BlockSpec((None, N_HEADS, HEAD_DIM, 1), lambda b, t: (b, 0, 0, 0)),
            pl.BlockSpec((None, N_HEADS, 1, 1), lambda b, t: (b, 0, 0, 0)),
            pl.BlockSpec((None, POOL_BUF, POOL_WIDTH), lambda b, t: (b, 0, 0)),
            pl.BlockSpec((up_rows, D_FF), lambda b, t: (b * nt + t, 0)),
            pl.BlockSpec((dn_rows, D_MODEL), lambda b, t: (b * nt + t, 0)),
            pl.BlockSpec((bt, N_HEADS, HEAD_DIM, HEAD_DIM), lambda b, t: (b * nt + t, 0, 0, 0)),
            pl.BlockSpec((bt, MLSTM_WIDTH), lambda b, t: (b * nt + t, 0)),
        ],
        out_shape=[
            jax.ShapeDtypeStruct((bsz * seq, D_MODEL), F32),
            jax.ShapeDtypeStruct((bsz, N_HEADS, HEAD_DIM, HEAD_DIM), F32),
            jax.ShapeDtypeStruct((bsz, N_HEADS, HEAD_DIM, 1), F32),
            jax.ShapeDtypeStruct((bsz, N_HEADS, 1, 1), F32),
            jax.ShapeDtypeStruct((bsz, POOL_BUF, POOL_WIDTH), F32),
            jax.ShapeDtypeStruct((D_MODEL, D_FF), BF16),
            jax.ShapeDtypeStruct((D_FF, D_MODEL), BF16),
            jax.ShapeDtypeStruct(c0_s.shape, F32),
            jax.ShapeDtypeStruct((n_s, MLSTM_WIDTH), F32),
        ],
        scratch_shapes=[
            pltpu.VMEM((D_MODEL, 3 * MLSTM_WIDTH + POOL_WIDTH), BF16),
            pltpu.VMEM((MLSTM_WIDTH, D_MODEL), BF16),
            pltpu.VMEM((GATE_ROWS, D_MODEL), BF16),
            pltpu.VMEM((D_MODEL, D_MODEL), BF16),
            pltpu.SemaphoreType.DMA((tl // half,)),
            pltpu.VMEM((N_HEADS, HEAD_DIM, 2 * HEAD_DIM), F32),
            pltpu.VMEM((N_HEADS, 1, 1), F32),
            pltpu.VMEM((POOL_HIST, POOL_WIDTH), F32),
        ],
        compiler_params=pltpu.CompilerParams(
            dimension_semantics=("arbitrary", "arbitrary"), vmem_limit_bytes=VMEM_LIMIT),
        name="mix_prompt",
    )(x, mod_p, g_pre1, g_post1, w_in_t, b_ig, b_fg, g_head, w_pool, pool_scale, w_out, w_up, w_dn,
      z_s, m0_s, b_ig, b_fg, c0_s)


def _mlp_rows(x_ref, y_ref, mod, gpre, gpost, wup_s, wdn_s, sub):
    sh2 = mod[:, 3 * D_MODEL:4 * D_MODEL]
    sc2 = mod[:, 4 * D_MODEL:5 * D_MODEL]
    ga2 = mod[:, 5 * D_MODEL:6 * D_MODEL]
    n_sub = x_ref.shape[0] // sub
    per_row = mod.shape[0] > 1

    def rows(r, a):
        return a[r * sub:(r + 1) * sub] if per_row else a

    hbs = []
    for r in range(n_sub):
        x = x_ref[r * sub:(r + 1) * sub, :]
        hbs.append(((_rms(x) * gpre) * (1.0 + rows(r, sc2)) + rows(r, sh2)).astype(BF16))
    accs = [jnp.zeros((sub, D_MODEL), F32) for _ in range(n_sub)]
    n_ff = D_FF // FF_CHUNK
    for j in range(n_ff):
        for r in range(n_sub):
            f = _dot(hbs[r], wup_s[:, j * FF_CHUNK:(j + 1) * FF_CHUNK])
            f = jnp.square(jnp.maximum(f, 0.0))
            accs[r] = accs[r] + _dot(f.astype(BF16), wdn_s[j * FF_CHUNK:(j + 1) * FF_CHUNK, :])
            if j == n_ff - 1:
                x = x_ref[r * sub:(r + 1) * sub, :]
                y = x + rows(r, ga2) * (_rms(accs[r]) * gpost)
                if len(y_ref.shape) == 3:
                    y_ref[r * sub:(r + 1) * sub, 0, :] = y
                else:
                    y_ref[r * sub:(r + 1) * sub, :] = y


def _mlp_kernel(xp_ref, xs_ref, modp_ref, mods_ref, gpre_ref, gpost_ref, wup_ref, wdn_ref,
                yp_ref, ys_ref, *, steps_per_mod):
    i = pl.program_id(0)
    n_prompt = pl.num_programs(0) - 1

    @pl.when(i < n_prompt)
    def _():
        mod = modp_ref[pl.ds(i // steps_per_mod, 1), :]
        _mlp_rows(xp_ref, yp_ref, mod, gpre_ref[...], gpost_ref[...], wup_ref, wdn_ref, MLP_SUB)

    @pl.when(i == n_prompt)
    def _():
        _mlp_rows(xs_ref, ys_ref, mods_ref[...], gpre_ref[...], gpost_ref[...], wup_ref, wdn_ref, xs_ref.shape[0])


def _mlp(xp, xs, mod_p, mod_s, rows_per_mod, g_pre2, g_post2, w_up_b, w_dn_b):
    n_p = xp.shape[0]
    n_s = xs.shape[0]
    tm = MLP_ROWS
    n_tiles = n_p // tm
    last = n_tiles - 1
    const2 = lambda i: (0, 0)
    ptile = lambda i: (jnp.minimum(i, last), 0)
    once = pl.Buffered(1)
    return pl.pallas_call(
        functools.partial(_mlp_kernel, steps_per_mod=rows_per_mod // tm),
        grid=(n_tiles + 1,),
        in_specs=[
            pl.BlockSpec((tm, D_MODEL), ptile),
            pl.BlockSpec((n_s, D_MODEL), const2, pipeline_mode=once),
            pl.BlockSpec(mod_p.shape, const2, pipeline_mode=once),
            pl.BlockSpec((n_s, 6 * D_MODEL), const2, pipeline_mode=once),
            pl.BlockSpec((1, D_MODEL), const2),
            pl.BlockSpec((1, D_MODEL), const2),
            pl.BlockSpec(w_up_b.shape, const2, pipeline_mode=once),
            pl.BlockSpec(w_dn_b.shape, const2, pipeline_mode=once),
        ],
        out_specs=[
            pl.BlockSpec((tm, D_MODEL), ptile),
            pl.BlockSpec((n_s, 1, D_MODEL), lambda i: (0, 0, 0)),
        ],
        out_shape=[
            jax.ShapeDtypeStruct((n_p, D_MODEL), F32),
            jax.ShapeDtypeStruct((n_s, 1, D_MODEL), F32),
        ],
        compiler_params=pltpu.CompilerParams(
            dimension_semantics=("arbitrary",), vmem_limit_bytes=VMEM_LIMIT),
        name="mlp",
    )(xp, xs, mod_p, mod_s, g_pre2, g_post2, w_up_b, w_dn_b)


def _sproj_kernel(x_ref, mod_ref, gpre_ref, w_ref, z_ref):
    x = x_ref[:, 0, :]
    mod = mod_ref[...]
    sh1 = mod[:, 0:D_MODEL]
    sc1 = mod[:, D_MODEL:2 * D_MODEL]
    hn = (_rms(x) * gpre_ref[...]) * (1.0 + sc1) + sh1
    z_ref[...] = _dot_nt(hn.astype(BF16), w_ref[...].astype(BF16))


def _sample_proj(x, mod_s, g_pre1, w_in_t):
    n = x.shape[0]
    return pl.pallas_call(
        _sproj_kernel,
        out_shape=jax.ShapeDtypeStruct((n, IN_COLS), F32),
        compiler_params=pltpu.CompilerParams(vmem_limit_bytes=VMEM_LIMIT),
        name="sample_proj",
    )(x, mod_s, g_pre1, w_in_t)


def _sample_gates(gates, m0):
    ig = gates[:, 0:N_HEADS]
    logf = jax.nn.log_sigmoid(gates[:, N_HEADS:2 * N_HEADS])
    g = logf + m0
    m = jnp.maximum(g, ig)
    return m, jnp.exp(ig - m), jnp.exp(g - m)


def _smix_kernel(x_ref, z_ref, qc_ref, mod_ref, n0_ref, m0_ref, pool_ref, big_ref, bfg_ref, ghead_ref,
                 wpool_ref, pscale_ref, wout_ref, gpost_ref,
                 x1_ref, n1_ref, m1_ref, poolo_ref):
    x = x_ref[:, 0, :]
    z = z_ref[...]
    mod = mod_ref[...]
    ga1 = mod[:, 2 * D_MODEL:3 * D_MODEL]
    gates = z[:, _CG:_CG + 2 * N_HEADS] + jnp.concatenate([big_ref[...], bfg_ref[...]], axis=1)
    m, w, a = _sample_gates(gates, m0_ref[...])
    m1_ref[...] = m
    ghead = ghead_ref[...]
    heads = []
    for h in range(N_HEADS):
        sl = slice(h * HEAD_DIM, (h + 1) * HEAD_DIM)
        q = z[:, _CQ + h * HEAD_DIM:_CQ + (h + 1) * HEAD_DIM]
        k = z[:, _CK + h * HEAD_DIM:_CK + (h + 1) * HEAD_DIM] * K_SCALE
        v = z[:, _CV + h * HEAD_DIM:_CV + (h + 1) * HEAD_DIM]
        o = z[:, _CO + h * HEAD_DIM:_CO + (h + 1) * HEAD_DIM]
        n0 = n0_ref[:, h, :]
        ah = a[:, h:h + 1]
        wh = w[:, h:h + 1]
        s = jnp.sum(q * k, axis=-1, keepdims=True) * wh
        num = ah * qc_ref[:, sl] + s * v
        den = ah * jnp.sum(q * n0, axis=-1, keepdims=True) + s
        hh = num / jnp.maximum(jnp.abs(den), jnp.exp(-m[:, h:h + 1]))
        n1_ref[:, h, :] = ah * n0 + wh * k
        hh = (_rms(hh) * ghead) * jax.nn.sigmoid(o)
        heads.append(hh.astype(BF16))

    u = z[:, _CU:_CU + POOL_WIDTH]
    g = POOL_GROUP_DIM
    pouts = []
    for gi, win in enumerate(POOL_WINDOWS):
        ug = u[:, gi * g:(gi + 1) * g]
        wsum = ug
        for r in range(POOL_BUF - (win - 1), POOL_BUF):
            wsum = wsum + pool_ref[r, :, gi * g:(gi + 1) * g]
        cnt = min(PAST_LEN + 1.0, float(win))
        pooled = wsum / cnt - ug
        pouts.append(_dot(pooled.astype(BF16), wpool_ref[gi].astype(BF16)))
    p_out = jnp.concatenate(pouts, axis=1) * pscale_ref[...]
    poolo_ref[0:POOL_BUF - 1] = pool_ref[1:POOL_BUF]
    poolo_ref[POOL_BUF - 1] = u

    mixh = jnp.concatenate(heads, axis=1)
    mix = (_dot(mixh, wout_ref[0:MLSTM_WIDTH, :].astype(BF16))
           + _dot(p_out.astype(BF16), wout_ref[MLSTM_WIDTH:, :].astype(BF16)))
    x1_ref[...] = x + ga1 * (_rms(mix) * gpost_ref[...])


def _sample_mix(x, z_s, qc, mod_s, n0, m0, pool_rows, b_ig, b_fg, g_head, w_pool, pool_scale, w_out, g_post1):
    n = x.shape[0]
    return pl.pallas_call(
        _smix_kernel,
        out_shape=[
            jax.ShapeDtypeStruct((n, D_MODEL), F32),
            jax.ShapeDtypeStruct((n, N_HEADS, HEAD_DIM), F32),
            jax.ShapeDtypeStruct((n, N_HEADS), F32),
            jax.ShapeDtypeStruct(pool_rows.shape, F32),
        ],
        compiler_params=pltpu.CompilerParams(vmem_limit_bytes=VMEM_LIMIT),
        name="sample_mix",
    )(x, z_s, qc, mod_s, n0, m0, pool_rows, b_ig, b_fg, g_head, w_pool, pool_scale, w_out, g_post1)


def kernel(x_prompt, x_sample, c_prompt, c_sample, state_C, state_n, state_m, state_pool, w_ada, b_ada,
           g_pre1, g_post1, w_in, b_ig, b_fg, g_head, w_pool, pool_scale, w_out, g_pre2, g_post2,
           w_up, w_down):
    depth = w_ada.shape[0]
    assert depth == 1, "single-layer step"
    bsz, seq, _ = x_prompt.shape
    nb = x_sample.shape[0]
    assert x_sample.shape[1] == 1 and seq % MIX_TOKENS == 0 and seq % MLP_ROWS == 0
    assert w_in.shape[2] == IN_COLS
    l = 0

    gpre1 = g_pre1[l][None, :]
    gpost1 = g_post1[l][None, :]
    gpre2 = g_pre2[l][None, :]
    gpost2 = g_post2[l][None, :]
    ghead = g_head[l][None, :]
    pscale = pool_scale[l][None, :]
    w_in_t = jnp.transpose(w_in[l])

    mod_p, mod_s = _ada(c_prompt, c_sample, w_ada[l], b_ada[l][None, :])

    m0 = state_m[l]
    z_s = _sample_proj(x_sample, mod_s, gpre1, w_in_t)

    x1p, c_p, n_p, m_p, pool_p, w_up_b, w_dn_b, c_s, qc = _mix_prompt(
        x_prompt, mod_p, gpre1, gpost1, w_in_t, b_ig, b_fg, ghead, w_pool[l], pscale, w_out, w_up, w_down,
        z_s, m0, state_C[l])

    x1s, n_s, m_s, pool_s = _sample_mix(
        x_sample, z_s, qc, mod_s, state_n[l], m0,
        jnp.transpose(state_pool[l], (1, 0, 2)), b_ig, b_fg, ghead, w_pool[l], pscale,
        w_out[l], gpost1)

    y_p, y_s = _mlp(x1p, x1s, mod_p, mod_s, seq, gpre2, gpost2, w_up_b, w_dn_b)

    return (y_p.reshape(bsz, seq, D_MODEL), y_s,
            c_p[None], n_p.reshape(1, bsz, N_HEADS, HEAD_DIM), m_p.reshape(1, bsz, N_HEADS), pool_p[None],
            c_s[None], n_s[None], m_s[None],
            jnp.transpose(pool_s, (1, 0, 2))[None])
```

```python
import functools

import jax
import jax.numpy as jnp
from jax import lax
from jax.experimental import pallas as pl
from jax.experimental.pallas import tpu as pltpu

F32 = jnp.float32
BF16 = jnp.bfloat16

D_MODEL = 1024
N_HEADS = 4
HEAD_DIM = 128
MLSTM_WIDTH = N_HEADS * HEAD_DIM
POOL_WIDTH = 512
POOL_WINDOWS = (2, 4, 8, 16)
POOL_GROUP_DIM = 128
POOL_BUF = 15
POOL_HIST = 16
D_FF = 4 * D_MODEL
EPS = 1e-6
PAST_LEN = 16384
K_SCALE = HEAD_DIM ** -0.5
GATE_ROWS = 16

_CQ, _CK, _CV, _CO = 0, 512, 1024, 1536
_CG = 2048
_CU = _CG + 2 * N_HEADS
IN_COLS = _CU + POOL_WIDTH

MIX_TOKENS = 1024
CHUNK = 256
MLP_ROWS = 1024
MLP_SUB = 512
FF_CHUNK = 1024
ADA_COLS = 1024
STAGE_ROWS = 256
VMEM_LIMIT = 60 * 1024 * 1024


def _dot(a, b):
    return jnp.dot(a, b, preferred_element_type=F32)


def _dot_nt(a, b):
    return lax.dot_general(a, b, (((1,), (1,)), ((), ())), preferred_element_type=F32)


def _rms(x):
    return x * lax.rsqrt(jnp.mean(x * x, axis=-1, keepdims=True) + EPS)


def _stream_cast(srcs, slot, n_slots, sem, sink):
    ahead = n_slots - 1

    def copy(i):
        return pltpu.make_async_copy(srcs[i], slot(i % n_slots), sem.at[i % n_slots])

    for i in range(min(ahead, len(srcs))):
        copy(i).start()
    for i in range(len(srcs)):
        if i + ahead < len(srcs):
            copy(i + ahead).start()
        copy(i).wait()
        sink(i, slot(i % n_slots)[...])


def _ada_kernel(cp_ref, cs_ref, w_ref, b_ref, op_ref, os_ref):
    c = jnp.concatenate([cp_ref[...], cs_ref[...]], axis=0)
    s = c * jax.nn.sigmoid(c)
    mod = _dot(s.astype(BF16), w_ref[...].astype(BF16)) + b_ref[...]
    n_p = cp_ref.shape[0]
    op_ref[...] = mod[0:n_p]
    os_ref[...] = mod[n_p:]


def _ada(c_p, c_s, w_ada, b_ada):
    n_p = c_p.shape[0]
    n_s = c_s.shape[0]
    tn = ADA_COLS
    return pl.pallas_call(
        _ada_kernel,
        grid=(6 * D_MODEL // tn,),
        in_specs=[
            pl.BlockSpec((n_p, D_MODEL), lambda j: (0, 0)),
            pl.BlockSpec((n_s, D_MODEL), lambda j: (0, 0)),
            pl.BlockSpec((D_MODEL, tn), lambda j: (0, j)),
            pl.BlockSpec((1, tn), lambda j: (0, j)),
        ],
        out_specs=[
            pl.BlockSpec((n_p, tn), lambda j: (0, j)),
            pl.BlockSpec((n_s, tn), lambda j: (0, j)),
        ],
        out_shape=[
            jax.ShapeDtypeStruct((n_p, 6 * D_MODEL), F32),
            jax.ShapeDtypeStruct((n_s, 6 * D_MODEL), F32),
        ],
        compiler_params=pltpu.CompilerParams(
            dimension_semantics=("arbitrary",), vmem_limit_bytes=VMEM_LIMIT),
        name="ada_mod",
    )(c_p, c_s, w_ada, b_ada)


def _cumsum_lanes(x, upper):
    hi = x.astype(BF16)
    r1 = x - hi.astype(F32)
    mid = r1.astype(BF16)
    lo = (r1 - mid.astype(F32)).astype(BF16)
    y = _dot(jnp.concatenate([hi, mid, lo], axis=0), upper)
    n = x.shape[0]
    return y[0:n] + y[n:2 * n] + y[2 * n:3 * n]


def _mix_kernel(x_ref, mod_ref, gpre_ref, gpost_ref, winT_hbm, big_ref, bfg_ref, ghead_ref,
                wpool_ref, pscale_ref, wout_hbm, wup_ref, wdn_ref, zs_ref, sm0_ref, sbig_ref, sbfg_ref, sc0_ref,
                x1_ref, c_ref, n_ref, m_ref, pool_ref, wupb_ref, wdnb_ref, sc1_ref, sqc_ref,
                wtm_s, wkt_s, wgt_s, wout_s, sem, cn_s, m_s, hist_s):
    tl = MIX_TOKENS
    lc = CHUNK
    n_chunks = tl // lc
    lane_tiles = lc // HEAD_DIM
    b = pl.program_id(0)
    t = pl.program_id(1)
    nt = pl.num_programs(1)

    @pl.when((b == 0) & (t == 0))
    def _():
        rows = STAGE_ROWS
        n_slots = tl // rows

        def slot(k):
            return x1_ref.at[pl.ds(k * rows, rows), :]

        groups = (_CQ, _CK, _CV, _CO, _CU)
        per_group = MLSTM_WIDTH // rows
        srcs = [winT_hbm.at[pl.ds(c + r * rows, rows), :] for c in groups for r in range(per_group)]
        n_in = len(srcs)
        srcs += [wout_hbm.at[0, pl.ds(r, rows), :] for r in range(0, D_MODEL, rows)]
        tm_col = {_CQ: 0, _CV: MLSTM_WIDTH, _CO: 2 * MLSTM_WIDTH, _CU: 3 * MLSTM_WIDTH}

        def sink(i, blk):
            if i < n_in:
                grp, off = groups[i // per_group], (i % per_group) * rows
                if grp == _CK:
                    wkt_s[off:off + rows, :] = blk.astype(BF16)
                else:
                    wtm_s[:, tm_col[grp] + off:tm_col[grp] + off + rows] = blk.T.astype(BF16)
            else:
                off = (i - n_in) * rows
                wout_s[off:off + rows, :] = blk.astype(BF16)

        _stream_cast(srcs, slot, n_slots, sem, sink)
        gate_dst = x1_ref.at[pl.ds(0, 2 * N_HEADS), :]
        gate_copy = pltpu.make_async_copy(winT_hbm.at[pl.ds(_CG, 2 * N_HEADS), :], gate_dst, sem.at[0])
        gate_copy.start()
        gate_copy.wait()
        pad = jnp.zeros((GATE_ROWS - 2 * N_HEADS, D_MODEL), F32)
        wgt_s[...] = jnp.concatenate([x1_ref[0:2 * N_HEADS, :], pad], axis=0).astype(BF16)
        for gi in range(len(POOL_WINDOWS)):
            cols = slice(gi * POOL_GROUP_DIM, (gi + 1) * POOL_GROUP_DIM)
            wrows = slice(MLSTM_WIDTH + gi * POOL_GROUP_DIM, MLSTM_WIDTH + (gi + 1) * POOL_GROUP_DIM)
            wg = (wpool_ref[gi] * pscale_ref[:, cols]).astype(BF16)
            wout_s[wrows, :] = _dot(wg, wout_s[wrows, :]).astype(BF16)

    @pl.when(t == 0)
    def _():
        cn_s[...] = jnp.zeros_like(cn_s)
        m_s[...] = jnp.zeros_like(m_s)
        hist_s[...] = jnp.zeros_like(hist_s)

    wupb_ref[...] = wup_ref[...].astype(BF16)
    wdnb_ref[...] = wdn_ref[...].astype(BF16)

    gate_row = lax.broadcasted_iota(jnp.int32, (GATE_ROWS, 1), 0)
    gate_bias = jnp.zeros((GATE_ROWS, 1), F32)
    for h in range(N_HEADS):
        gate_bias = jnp.where(gate_row == h, big_ref[0, h], gate_bias)
        gate_bias = jnp.where(gate_row == N_HEADS + h, bfg_ref[0, h], gate_bias)

    mod = mod_ref[pl.ds(b, 1), :]
    sh1 = mod[:, 0:D_MODEL]
    sc1 = mod[:, D_MODEL:2 * D_MODEL]
    ga1 = mod[:, 2 * D_MODEL:3 * D_MODEL]
    gpre = gpre_ref[...]
    gpost = gpost_ref[...]
    ghead = ghead_ref[...]
    row_i = lax.broadcasted_iota(jnp.int32, (lc, lc), 0)
    col_i = lax.broadcasted_iota(jnp.int32, (lc, lc), 1)
    causal = col_i <= row_i
    upper = (row_i <= col_i).astype(BF16)
    ones_blk = jnp.ones((lc, HEAD_DIM), BF16)
    g = POOL_GROUP_DIM

    def norm_and_gates(c):
        x = x_ref[c * lc:(c + 1) * lc, :]
        hnb = ((_rms(x) * gpre) * (1.0 + sc1) + sh1).astype(BF16)
        gt = _dot_nt(wgt_s[...], hnb) + gate_bias
        b16 = _cumsum_lanes(jax.nn.log_sigmoid(gt), upper)
        bcs = b16[N_HEADS:2 * N_HEADS]
        return hnb, gt[0:N_HEADS] - bcs, bcs, b16.T

    def project(hnb):
        ztm = _dot(hnb, wtm_s[...])
        kt_all = _dot_nt(wkt_s[...], hnb)
        return ztm, kt_all

    cns = [cn_s[h] for h in range(N_HEADS)]
    ms = [m_s[h] for h in range(N_HEADS)]
    hist = [hist_s[...]]
    last_ext = [None]

    def heads(ztm, kt_all, rr, bcs, bcs_col):
        outs = []
        for h in range(N_HEADS):
            sl = slice(h * HEAD_DIM, (h + 1) * HEAD_DIM)
            q = ztm[:, sl]
            v = ztm[:, MLSTM_WIDTH + h * HEAD_DIM:MLSTM_WIDTH + (h + 1) * HEAD_DIM]
            o = ztm[:, 2 * MLSTM_WIDTH + h * HEAD_DIM:2 * MLSTM_WIDTH + (h + 1) * HEAD_DIM]
            kt = kt_all[sl, :] * K_SCALE
            r_row = rr[h:h + 1, :]
            m0 = ms[h]
            cn = cns[h]

            rmat = jnp.where(causal, r_row, -jnp.inf)
            mcol = jnp.maximum(jnp.max(rmat, axis=-1, keepdims=True), m0)
            mcol_b = jnp.broadcast_to(mcol, (lc, HEAD_DIM))
            wmat = jnp.exp(rmat - jnp.concatenate([mcol_b] * lane_tiles, axis=1))
            a_b = jnp.exp(m0 - mcol_b)
            bcol = bcs_col[:, N_HEADS + h:N_HEADS + h + 1]

            qb = q.astype(BF16)
            vext = jnp.concatenate([v.astype(BF16), ones_blk], axis=1)
            s = _dot(qb, kt.astype(BF16)) * wmat
            qcn = _dot(qb, cn.astype(BF16))
            sv = _dot(s.astype(BF16), vext)
            nd = jnp.concatenate([a_b, a_b], axis=1) * qcn + sv
            num = nd[:, 0:HEAD_DIM]
            den = nd[:, HEAD_DIM:]
            hh = num / jnp.maximum(jnp.abs(den), jnp.exp(-(bcol + mcol)))
            hh = (_rms(hh) * ghead) * jax.nn.sigmoid(o)
            outs.append(hh.astype(BF16))

            ml = mcol[lc - 1:lc, :]
            al = jnp.exp(m0 - ml)
            wl = jnp.exp(r_row - ml)
            kw = (kt * wl).astype(BF16)
            cns[h] = al * cn + _dot(kw, vext)
            ms[h] = bcs[h:h + 1, lc - 1:lc] + ml
        return jnp.concatenate(outs, axis=1)

    def pool(c, u):
        ext = jnp.concatenate([hist[0], u], axis=0)
        p2 = ext + pltpu.roll(ext, 1, axis=0)
        p4 = p2[:, g:] + pltpu.roll(p2[:, g:], 2, axis=0)
        p8 = p4[:, g:] + pltpu.roll(p4[:, g:], 4, axis=0)
        p16 = p8[:, g:] + pltpu.roll(p8[:, g:], 8, axis=0)
        wsum = (p2[:, 0:g], p4[:, 0:g], p8[:, 0:g], p16)
        pos = (t * tl + c * lc + lax.broadcasted_iota(jnp.int32, (lc, 1), 0)).astype(F32)
        pouts = []
        for gi, win in enumerate(POOL_WINDOWS):
            cnt = jnp.minimum(pos + 1.0, float(win))
            ug = u[:, gi * g:(gi + 1) * g]
            pooled = wsum[gi][POOL_HIST:, :] / cnt - ug
            pouts.append(pooled.astype(BF16))
        hist[0] = ext[lc:, :]
        last_ext[0] = ext
        return jnp.concatenate(pouts, axis=1)

    def finish(c, mixh, p_out):
        mix = _dot(mixh, wout_s[0:MLSTM_WIDTH, :]) + _dot(p_out, wout_s[MLSTM_WIDTH:, :])
        x = x_ref[c * lc:(c + 1) * lc, :]
        x1_ref[c * lc:(c + 1) * lc, :] = x + ga1 * (_rms(mix) * gpost)

    pres = []
    zs = []
    z_smp = zs_ref[...]
    bt = z_smp.shape[0]
    smp_gates = z_smp[:, _CG:_CG + 2 * N_HEADS] + jnp.concatenate([sbig_ref[...], sbfg_ref[...]], axis=1)
    _, smp_w, smp_a = _sample_gates(smp_gates, sm0_ref[...])

    def sample_memory_update(h):
        sl = slice(h * HEAD_DIM, (h + 1) * HEAD_DIM)
        q_t = z_smp[:, _CQ + h * HEAD_DIM:_CQ + (h + 1) * HEAD_DIM].T
        kw_t = (z_smp[:, _CK + h * HEAD_DIM:_CK + (h + 1) * HEAD_DIM] * K_SCALE * smp_w[:, h:h + 1]).T
        rows = []
        for j in range(bt):
            c0 = sc0_ref[j, h]
            v = z_smp[j:j + 1, _CV + h * HEAD_DIM:_CV + (h + 1) * HEAD_DIM]
            sc1_ref[j, h] = smp_a[j:j + 1, h:h + 1] * c0 + kw_t[:, j:j + 1] * v
            rows.append(jnp.sum(q_t[:, j:j + 1] * c0, axis=0, keepdims=True))
        sqc_ref[:, sl] = jnp.concatenate(rows, axis=0)

    for c in range(n_chunks):
        pres.append(norm_and_gates(c))
        zs.append(project(pres[c][0]))
        for h in range(c, N_HEADS, n_chunks):
            sample_memory_update(h)
    mixhs = [heads(zs[c][0], zs[c][1], *pres[c][1:]) for c in range(n_chunks)]
    p_outs = [pool(c, zs[c][0][:, 3 * MLSTM_WIDTH:3 * MLSTM_WIDTH + POOL_WIDTH]) for c in range(n_chunks)]
    for c in range(n_chunks):
        finish(c, mixhs[c], p_outs[c])

    for h in range(N_HEADS):
        cn_s[h] = cns[h]
        m_s[h] = ms[h]
    hist_s[...] = hist[0]

    @pl.when(t == nt - 1)
    def _():
        for h in range(N_HEADS):
            c_ref[h] = cns[h][:, 0:HEAD_DIM]
            n_ref[h:h + 1, :] = cns[h][:, HEAD_DIM:].T[0:1, :]
            m_ref[h] = ms[h]
        pool_ref[...] = last_ext[0][lc + POOL_HIST - POOL_BUF:, :]


def _mix_prompt(x, mod_p, g_pre1, g_post1, w_in_t, b_ig, b_fg, g_head, w_pool, pool_scale, w_out, w_up, w_dn,
                z_s, m0_s, c0_s):
    bsz, seq, _ = x.shape
    n_s = z_s.shape[0]
    tl = MIX_TOKENS
    nt = seq // tl
    steps = bsz * nt
    assert D_MODEL % (16 * steps) == 0
    assert n_s % (8 * steps) == 0
    assert tl % STAGE_ROWS == 0 and MLSTM_WIDTH % STAGE_ROWS == 0
    up_rows = D_MODEL // steps
    dn_rows = D_FF // steps
    bt = n_s // steps
    const2 = lambda b, t: (0, 0)
    const3 = lambda b, t: (0, 0, 0)
    share = lambda b, t: (0, b * nt + t, 0)
    smem = pl.BlockSpec(memory_space=pltpu.SMEM)
    return pl.pallas_call(
        _mix_kernel,
        grid=(bsz, nt),
        in_specs=[
            pl.BlockSpec((None, tl, D_MODEL), lambda b, t: (b, t, 0)),
            pl.BlockSpec(mod_p.shape, const2),
            pl.BlockSpec((1, D_MODEL), const2),
            pl.BlockSpec((1, D_MODEL), const2),
            pl.BlockSpec(memory_space=pl.ANY),
            smem,
            smem,
            pl.BlockSpec((1, HEAD_DIM), const2),
            pl.BlockSpec(w_pool.shape, const3),
            pl.BlockSpec((1, POOL_WIDTH), const2),
            pl.BlockSpec(memory_space=pl.ANY),
            pl.BlockSpec((None, up_rows, D_FF), share),
            pl.BlockSpec((None, dn_rows, D_MODEL), share),
            pl.BlockSpec((bt, z_s.shape[1]), lambda b, t: (b * nt + t, 0)),
            pl.BlockSpec((bt, N_HEADS), lambda b, t: (b * nt + t, 0)),
            pl.BlockSpec((1, N_HEADS), const2),
            pl.BlockSpec((1, N_HEADS), const2),
            pl.BlockSpec((bt, N_HEADS, HEAD_DIM, HEAD_DIM), lambda b, t: (b * nt + t, 0, 0, 0)),
        ],
        out_specs=[
            pl.BlockSpec((tl, D_MODEL), lambda b, t: (b * nt + t, 0)),
            pl.BlockSpec((None, N_HEADS, HEAD_DIM, HEAD_DIM), lambda b, t: (b, 0, 0, 0)),
            pl.BlockSpec((None, N_HEADS, HEAD_DIM), lambda b, t: (b, 0, 0)),
            pl.BlockSpec((None, N_HEADS, 1, 1), lambda b, t: (b, 0, 0, 0)),
            pl.BlockSpec((None, POOL_BUF, POOL_WIDTH), lambda b, t: (b, 0, 0)),
            pl.BlockSpec((up_rows, D_FF), lambda b, t: (b * nt + t, 0)),
            pl.BlockSpec((dn_rows, D_MODEL), lambda b, t: (b * nt + t, 0)),
            pl.BlockSpec((bt, N_HEADS, HEAD_DIM, HEAD_DIM), lambda b, t: (b * nt + t, 0, 0, 0)),
            pl.BlockSpec((bt, MLSTM_WIDTH), lambda b, t: (b * nt + t, 0)),
        ],
        out_shape=[
            jax.ShapeDtypeStruct((bsz * seq, D_MODEL), F32),
            jax.ShapeDtypeStruct((bsz, N_HEADS, HEAD_DIM, HEAD_DIM), F32),
            jax.ShapeDtypeStruct((bsz, N_HEADS, HEAD_DIM), F32),
            jax.ShapeDtypeStruct((bsz, N_HEADS, 1, 1), F32),
            jax.ShapeDtypeStruct((bsz, POOL_BUF, POOL_WIDTH), F32),
            jax.ShapeDtypeStruct((D_MODEL, D_FF), BF16),
            jax.ShapeDtypeStruct((D_FF, D_MODEL), BF16),
            jax.ShapeDtypeStruct(c0_s.shape, F32),
            jax.ShapeDtypeStruct((n_s, MLSTM_WIDTH), F32),
        ],
        scratch_shapes=[
            pltpu.VMEM((D_MODEL, 3 * MLSTM_WIDTH + POOL_WIDTH), BF16),
            pltpu.VMEM((MLSTM_WIDTH, D_MODEL), BF16),
            pltpu.VMEM((GATE_ROWS, D_MODEL), BF16),
            pltpu.VMEM((D_MODEL, D_MODEL), BF16),
            pltpu.SemaphoreType.DMA((tl // STAGE_ROWS,)),
            pltpu.VMEM((N_HEADS, HEAD_DIM, 2 * HEAD_DIM), F32),
            pltpu.VMEM((N_HEADS, 1, 1), F32),
            pltpu.VMEM((POOL_HIST, POOL_WIDTH), F32),
        ],
        compiler_params=pltpu.CompilerParams(
            dimension_semantics=("arbitrary", "arbitrary"), vmem_limit_bytes=VMEM_LIMIT),
        name="mix_prompt",
    )(x, mod_p, g_pre1, g_post1, w_in_t, b_ig, b_fg, g_head, w_pool, pool_scale, w_out, w_up, w_dn,
      z_s, m0_s, b_ig, b_fg, c0_s)


def _mlp_rows(x_ref, y_ref, mod, gpre, gpost, wup_s, wdn_s, sub):
    sh2 = mod[:, 3 * D_MODEL:4 * D_MODEL]
    sc2 = mod[:, 4 * D_MODEL:5 * D_MODEL]
    ga2 = mod[:, 5 * D_MODEL:6 * D_MODEL]
    n_sub = x_ref.shape[0] // sub
    per_row = mod.shape[0] > 1

    def rows(r, a):
        return a[r * sub:(r + 1) * sub] if per_row else a

    hbs = []
    for r in range(n_sub):
        x = x_ref[r * sub:(r + 1) * sub, :]
        hbs.append(((_rms(x) * gpre) * (1.0 + rows(r, sc2)) + rows(r, sh2)).astype(BF16))
    accs = [jnp.zeros((sub, D_MODEL), F32) for _ in range(n_sub)]
    n_ff = D_FF // FF_CHUNK
    for j in range(n_ff):
        for r in range(n_sub):
            f = _dot(hbs[r], wup_s[:, j * FF_CHUNK:(j + 1) * FF_CHUNK])
            f = jnp.square(jnp.maximum(f, 0.0))
            accs[r] = accs[r] + _dot(f.astype(BF16), wdn_s[j * FF_CHUNK:(j + 1) * FF_CHUNK, :])
            if j == n_ff - 1:
                x = x_ref[r * sub:(r + 1) * sub, :]
                y = x + rows(r, ga2) * (_rms(accs[r]) * gpost)
                if len(y_ref.shape) == 3:
                    y_ref[r * sub:(r + 1) * sub, 0, :] = y
                else:
                    y_ref[r * sub:(r + 1) * sub, :] = y


def _mlp_kernel(xp_ref, xs_ref, modp_ref, mods_ref, gpre_ref, gpost_ref, wup_ref, wdn_ref,
                yp_ref, ys_ref, *, steps_per_mod):
    i = pl.program_id(0)
    n_prompt = pl.num_programs(0) - 1

    @pl.when(i < n_prompt)
    def _():
        mod = modp_ref[pl.ds(i // steps_per_mod, 1), :]
        _mlp_rows(xp_ref, yp_ref, mod, gpre_ref[...], gpost_ref[...], wup_ref, wdn_ref, MLP_SUB)

    @pl.when(i == n_prompt)
    def _():
        _mlp_rows(xs_ref, ys_ref, mods_ref[...], gpre_ref[...], gpost_ref[...], wup_ref, wdn_ref, xs_ref.shape[0])


def _mlp(xp, xs, mod_p, mod_s, rows_per_mod, g_pre2, g_post2, w_up_b, w_dn_b):
    n_p = xp.shape[0]
    n_s = xs.shape[0]
    tm = MLP_ROWS
    n_tiles = n_p // tm
    last = n_tiles - 1
    const2 = lambda i: (0, 0)
    ptile = lambda i: (jnp.minimum(i, last), 0)
    once = pl.Buffered(1)
    return pl.pallas_call(
        functools.partial(_mlp_kernel, steps_per_mod=rows_per_mod // tm),
        grid=(n_tiles + 1,),
        in_specs=[
            pl.BlockSpec((tm, D_MODEL), ptile),
            pl.BlockSpec((n_s, D_MODEL), const2, pipeline_mode=once),
            pl.BlockSpec(mod_p.shape, const2, pipeline_mode=once),
            pl.BlockSpec((n_s, 6 * D_MODEL), const2, pipeline_mode=once),
            pl.BlockSpec((1, D_MODEL), const2),
            pl.BlockSpec((1, D_MODEL), const2),
            pl.BlockSpec(w_up_b.shape, const2, pipeline_mode=once),
            pl.BlockSpec(w_dn_b.shape, const2, pipeline_mode=once),
        ],
        out_specs=[
            pl.BlockSpec((tm, D_MODEL), ptile),
            pl.BlockSpec((n_s, 1, D_MODEL), lambda i: (0, 0, 0)),
        ],
        out_shape=[
            jax.ShapeDtypeStruct((n_p, D_MODEL), F32),
            jax.ShapeDtypeStruct((n_s, 1, D_MODEL), F32),
        ],
        compiler_params=pltpu.CompilerParams(
            dimension_semantics=("arbitrary",), vmem_limit_bytes=VMEM_LIMIT),
        name="mlp",
    )(xp, xs, mod_p, mod_s, g_pre2, g_post2, w_up_b, w_dn_b)


def _sproj_kernel(x_ref, mod_ref, gpre_ref, w_ref, z_ref):
    x = x_ref[:, 0, :]
    mod = mod_ref[...]
    sh1 = mod[:, 0:D_MODEL]
    sc1 = mod[:, D_MODEL:2 * D_MODEL]
    hn = (_rms(x) * gpre_ref[...]) * (1.0 + sc1) + sh1
    z_ref[...] = _dot_nt(hn.astype(BF16), w_ref[...].astype(BF16))


def _sample_proj(x, mod_s, g_pre1, w_in_t):
    n = x.shape[0]
    return pl.pallas_call(
        _sproj_kernel,
        out_shape=jax.ShapeDtypeStruct((n, IN_COLS), F32),
        compiler_params=pltpu.CompilerParams(vmem_limit_bytes=VMEM_LIMIT),
        name="sample_proj",
    )(x, mod_s, g_pre1, w_in_t)


def _sample_gates(gates, m0):
    ig = gates[:, 0:N_HEADS]
    logf = jax.nn.log_sigmoid(gates[:, N_HEADS:2 * N_HEADS])
    g = logf + m0
    m = jnp.maximum(g, ig)
    return m, jnp.exp(ig - m), jnp.exp(g - m)


def _smix_kernel(x_ref, z_ref, qc_ref, mod_ref, n0_ref, m0_ref, pool_ref, big_ref, bfg_ref, ghead_ref,
                 wpool_ref, pscale_ref, wout_ref, gpost_ref,
                 x1_ref, n1_ref, m1_ref, poolo_ref):
    x = x_ref[:, 0, :]
    z = z_ref[...]
    mod = mod_ref[...]
    ga1 = mod[:, 2 * D_MODEL:3 * D_MODEL]
    gates = z[:, _CG:_CG + 2 * N_HEADS] + jnp.concatenate([big_ref[...], bfg_ref[...]], axis=1)
    m, w, a = _sample_gates(gates, m0_ref[...])
    m1_ref[...] = m
    ghead = ghead_ref[...]
    heads = []
    for h in range(N_HEADS):
        sl = slice(h * HEAD_DIM, (h + 1) * HEAD_DIM)
        q = z[:, _CQ + h * HEAD_DIM:_CQ + (h + 1) * HEAD_DIM]
        k = z[:, _CK + h * HEAD_DIM:_CK + (h + 1) * HEAD_DIM] * K_SCALE
        v = z[:, _CV + h * HEAD_DIM:_CV + (h + 1) * HEAD_DIM]
        o = z[:, _CO + h * HEAD_DIM:_CO + (h + 1) * HEAD_DIM]
        n0 = n0_ref[:, h, :]
        ah = a[:, h:h + 1]
        wh = w[:, h:h + 1]
        s = jnp.sum(q * k, axis=-1, keepdims=True) * wh
        num = ah * qc_ref[:, sl] + s * v
        den = ah * jnp.sum(q * n0, axis=-1, keepdims=True) + s
        hh = num / jnp.maximum(jnp.abs(den), jnp.exp(-m[:, h:h + 1]))
        n1_ref[:, h, :] = ah * n0 + wh * k
        hh = (_rms(hh) * ghead) * jax.nn.sigmoid(o)
        heads.append(hh.astype(BF16))

    u = z[:, _CU:_CU + POOL_WIDTH]
    g = POOL_GROUP_DIM
    pouts = []
    for gi, win in enumerate(POOL_WINDOWS):
        ug = u[:, gi * g:(gi + 1) * g]
        wsum = ug
        for r in range(POOL_BUF - (win - 1), POOL_BUF):
            wsum = wsum + pool_ref[r, :, gi * g:(gi + 1) * g]
        cnt = min(PAST_LEN + 1.0, float(win))
        pooled = wsum / cnt - ug
        pouts.append(_dot(pooled.astype(BF16), wpool_ref[gi].astype(BF16)))
    p_out = jnp.concatenate(pouts, axis=1) * pscale_ref[...]
    poolo_ref[0:POOL_BUF - 1] = pool_ref[1:POOL_BUF]
    poolo_ref[POOL_BUF - 1] = u

    mixh = jnp.concatenate(heads, axis=1)
    mix = (_dot(mixh, wout_ref[0:MLSTM_WIDTH, :].astype(BF16))
           + _dot(p_out.astype(BF16), wout_ref[MLSTM_WIDTH:, :].astype(BF16)))
    x1_ref[...] = x + ga1 * (_rms(mix) * gpost_ref[...])


def _sample_mix(x, z_s, qc, mod_s, n0, m0, pool_rows, b_ig, b_fg, g_head, w_pool, pool_scale, w_out, g_post1):
    n = x.shape[0]
    return pl.pallas_call(
        _smix_kernel,
        out_shape=[
            jax.ShapeDtypeStruct((n, D_MODEL), F32),
            jax.ShapeDtypeStruct((n, N_HEADS, HEAD_DIM), F32),
            jax.ShapeDtypeStruct((n, N_HEADS), F32),
            jax.ShapeDtypeStruct(pool_rows.shape, F32),
        ],
        compiler_params=pltpu.CompilerParams(vmem_limit_bytes=VMEM_LIMIT),
        name="sample_mix",
    )(x, z_s, qc, mod_s, n0, m0, pool_rows, b_ig, b_fg, g_head, w_pool, pool_scale, w_out, g_post1)


def kernel(x_prompt, x_sample, c_prompt, c_sample, state_C, state_n, state_m, state_pool, w_ada, b_ada,
           g_pre1, g_post1, w_in, b_ig, b_fg, g_head, w_pool, pool_scale, w_out, g_pre2, g_post2,
           w_up, w_down):
    depth = w_ada.shape[0]
    assert depth == 1, "single-layer step"
    bsz, seq, _ = x_prompt.shape
    nb = x_sample.shape[0]
    assert x_sample.shape[1] == 1 and seq % MIX_TOKENS == 0 and seq % MLP_ROWS == 0
    assert w_in.shape[2] == IN_COLS
    l = 0

    gpre1 = g_pre1[l][None, :]
    gpost1 = g_post1[l][None, :]
    gpre2 = g_pre2[l][None, :]
    gpost2 = g_post2[l][None, :]
    ghead = g_head[l][None, :]
    pscale = pool_scale[l][None, :]
    w_in_t = jnp.transpose(w_in[l])

    mod_p, mod_s = _ada(c_prompt, c_sample, w_ada[l], b_ada[l][None, :])

    m0 = state_m[l]
    z_s = _sample_proj(x_sample, mod_s, gpre1, w_in_t)

    x1p, c_p, n_p, m_p, pool_p, w_up_b, w_dn_b, c_s, qc = _mix_prompt(
        x_prompt, mod_p, gpre1, gpost1, w_in_t, b_ig, b_fg, ghead, w_pool[l], pscale, w_out, w_up, w_down,
        z_s, m0, state_C[l])

    x1s, n_s, m_s, pool_s = _sample_mix(
        x_sample, z_s, qc, mod_s, state_n[l], m0,
        jnp.transpose(state_pool[l], (1, 0, 2)), b_ig, b_fg, ghead, w_pool[l], pscale,
        w_out[l], gpost1)

    y_p, y_s = _mlp(x1p, x1s, mod_p, mod_s, seq, gpre2, gpost2, w_up_b, w_dn_b)

    return (y_p.reshape(bsz, seq, D_MODEL), y_s,
            c_p[None], n_p[None], m_p.reshape(1, bsz, N_HEADS), pool_p[None],
            c_s[None], n_s[None], m_s[None],
            jnp.transpose(pool_s, (1, 0, 2))[None])
```

```python
import functools

import jax
import jax.numpy as jnp
from jax import lax
from jax.experimental import pallas as pl
from jax.experimental.pallas import tpu as pltpu

F32 = jnp.float32
BF16 = jnp.bfloat16

D_MODEL = 1024
N_HEADS = 4
HEAD_DIM = 128
MLSTM_WIDTH = N_HEADS * HEAD_DIM
POOL_WIDTH = 512
POOL_WINDOWS = (2, 4, 8, 16)
POOL_GROUP_DIM = 128
POOL_BUF = 15
POOL_HIST = 16
D_FF = 4 * D_MODEL
EPS = 1e-6
PAST_LEN = 16384
K_SCALE = HEAD_DIM ** -0.5
GATE_ROWS = 16

_CQ, _CK, _CV, _CO = 0, 512, 1024, 1536
_CG = 2048
_CU = _CG + 2 * N_HEADS
IN_COLS = _CU + POOL_WIDTH

MIX_TOKENS = 1024
CHUNK = 256
MLP_ROWS = 1024
MLP_SUB = 512
FF_CHUNK = 1024
ADA_COLS = 1024
STAGE_ROWS = 256
VMEM_LIMIT = 60 * 1024 * 1024


def _dot(a, b):
    return jnp.dot(a, b, preferred_element_type=F32)


def _dot_nt(a, b):
    return lax.dot_general(a, b, (((1,), (1,)), ((), ())), preferred_element_type=F32)


def _rms(x):
    return x * lax.rsqrt(jnp.mean(x * x, axis=-1, keepdims=True) + EPS)


def _stream_cast(srcs, slot, n_slots, sem, sink):
    ahead = n_slots - 1

    def copy(i):
        return pltpu.make_async_copy(srcs[i], slot(i % n_slots), sem.at[i % n_slots])

    for i in range(min(ahead, len(srcs))):
        copy(i).start()
    for i in range(len(srcs)):
        if i + ahead < len(srcs):
            copy(i + ahead).start()
        copy(i).wait()
        sink(i, slot(i % n_slots)[...])


def _ada_kernel(cp_ref, cs_ref, w_ref, b_ref, op_ref, os_ref):
    c = jnp.concatenate([cp_ref[...], cs_ref[...]], axis=0)
    s = c * jax.nn.sigmoid(c)
    mod = _dot(s.astype(BF16), w_ref[...].astype(BF16)) + b_ref[...]
    n_p = cp_ref.shape[0]
    op_ref[...] = mod[0:n_p]
    os_ref[...] = mod[n_p:]


def _ada(c_p, c_s, w_ada, b_ada):
    n_p = c_p.shape[0]
    n_s = c_s.shape[0]
    tn = ADA_COLS
    return pl.pallas_call(
        _ada_kernel,
        grid=(6 * D_MODEL // tn,),
        in_specs=[
            pl.BlockSpec((n_p, D_MODEL), lambda j: (0, 0)),
            pl.BlockSpec((n_s, D_MODEL), lambda j: (0, 0)),
            pl.BlockSpec((D_MODEL, tn), lambda j: (0, j)),
            pl.BlockSpec((1, tn), lambda j: (0, j)),
        ],
        out_specs=[
            pl.BlockSpec((n_p, tn), lambda j: (0, j)),
            pl.BlockSpec((n_s, tn), lambda j: (0, j)),
        ],
        out_shape=[
            jax.ShapeDtypeStruct((n_p, 6 * D_MODEL), F32),
            jax.ShapeDtypeStruct((n_s, 6 * D_MODEL), F32),
        ],
        compiler_params=pltpu.CompilerParams(
            dimension_semantics=("arbitrary",), vmem_limit_bytes=VMEM_LIMIT),
        name="ada_mod",
    )(c_p, c_s, w_ada, b_ada)


def _cumsum_lanes(x, upper):
    hi = x.astype(BF16)
    r1 = x - hi.astype(F32)
    mid = r1.astype(BF16)
    lo = (r1 - mid.astype(F32)).astype(BF16)
    y = _dot(jnp.concatenate([hi, mid, lo], axis=0), upper)
    n = x.shape[0]
    return y[0:n] + y[n:2 * n] + y[2 * n:3 * n]


def _mix_kernel(x_ref, mod_ref, gpre_ref, gpost_ref, winT_hbm, big_ref, bfg_ref, ghead_ref,
                wpool_ref, pscale_ref, wout_hbm, wup_ref, wdn_ref, zs_ref, sm0_ref, sbig_ref, sbfg_ref, sc0_ref,
                x1_ref, c_ref, n_ref, m_ref, pool_ref, wupb_ref, wdnb_ref, sc1_ref, sqc_ref,
                wtm_s, wkt_s, wgt_s, wout_s, sem, cn_s, m_s, hist_s):
    tl = MIX_TOKENS
    lc = CHUNK
    n_chunks = tl // lc
    lane_tiles = lc // HEAD_DIM
    b = pl.program_id(0)
    t = pl.program_id(1)
    nt = pl.num_programs(1)

    @pl.when((b == 0) & (t == 0))
    def _():
        rows = STAGE_ROWS
        n_slots = tl // rows

        def slot(k):
            return x1_ref.at[pl.ds(k * rows, rows), :]

        groups = (_CQ, _CK, _CV, _CO, _CU)
        per_group = MLSTM_WIDTH // rows
        srcs = [winT_hbm.at[pl.ds(c + r * rows, rows), :] for c in groups for r in range(per_group)]
        n_in = len(srcs)
        srcs += [wout_hbm.at[0, pl.ds(r, rows), :] for r in range(0, D_MODEL, rows)]
        tm_col = {_CQ: 0, _CV: MLSTM_WIDTH, _CO: 2 * MLSTM_WIDTH, _CU: 3 * MLSTM_WIDTH}

        def sink(i, blk):
            if i < n_in:
                grp, off = groups[i // per_group], (i % per_group) * rows
                if grp == _CK:
                    wkt_s[off:off + rows, :] = blk.astype(BF16)
                else:
                    wtm_s[:, tm_col[grp] + off:tm_col[grp] + off + rows] = blk.T.astype(BF16)
            else:
                off = (i - n_in) * rows
                wout_s[off:off + rows, :] = blk.astype(BF16)

        _stream_cast(srcs, slot, n_slots, sem, sink)
        gate_dst = x1_ref.at[pl.ds(0, 2 * N_HEADS), :]
        gate_copy = pltpu.make_async_copy(winT_hbm.at[pl.ds(_CG, 2 * N_HEADS), :], gate_dst, sem.at[0])
        gate_copy.start()
        gate_copy.wait()
        pad = jnp.zeros((GATE_ROWS - 2 * N_HEADS, D_MODEL), F32)
        wgt_s[...] = jnp.concatenate([x1_ref[0:2 * N_HEADS, :], pad], axis=0).astype(BF16)
        for gi in range(len(POOL_WINDOWS)):
            cols = slice(gi * POOL_GROUP_DIM, (gi + 1) * POOL_GROUP_DIM)
            wrows = slice(MLSTM_WIDTH + gi * POOL_GROUP_DIM, MLSTM_WIDTH + (gi + 1) * POOL_GROUP_DIM)
            wg = (wpool_ref[gi] * pscale_ref[:, cols]).astype(BF16)
            wout_s[wrows, :] = _dot(wg, wout_s[wrows, :]).astype(BF16)

    @pl.when(t == 0)
    def _():
        cn_s[...] = jnp.zeros_like(cn_s)
        m_s[...] = jnp.zeros_like(m_s)
        hist_s[...] = jnp.zeros_like(hist_s)

    wupb_ref[...] = wup_ref[...].astype(BF16)
    wdnb_ref[...] = wdn_ref[...].astype(BF16)

    gate_row = lax.broadcasted_iota(jnp.int32, (GATE_ROWS, 1), 0)
    gate_bias = jnp.zeros((GATE_ROWS, 1), F32)
    for h in range(N_HEADS):
        gate_bias = jnp.where(gate_row == h, big_ref[0, h], gate_bias)
        gate_bias = jnp.where(gate_row == N_HEADS + h, bfg_ref[0, h], gate_bias)

    mod = mod_ref[pl.ds(b, 1), :]
    sh1 = mod[:, 0:D_MODEL]
    sc1 = mod[:, D_MODEL:2 * D_MODEL]
    ga1 = mod[:, 2 * D_MODEL:3 * D_MODEL]
    gpre = gpre_ref[...]
    gpost = gpost_ref[...]
    ghead = ghead_ref[...]
    row_i = lax.broadcasted_iota(jnp.int32, (lc, lc), 0)
    col_i = lax.broadcasted_iota(jnp.int32, (lc, lc), 1)
    causal = col_i <= row_i
    upper = (row_i <= col_i).astype(BF16)
    ones_blk = jnp.ones((lc, HEAD_DIM), BF16)
    g = POOL_GROUP_DIM

    def norm_and_gates(c):
        x = x_ref[c * lc:(c + 1) * lc, :]
        hnb = ((_rms(x) * gpre) * (1.0 + sc1) + sh1).astype(BF16)
        gt = _dot_nt(wgt_s[...], hnb) + gate_bias
        b16 = _cumsum_lanes(jax.nn.log_sigmoid(gt), upper)
        bcs = b16[N_HEADS:2 * N_HEADS]
        return hnb, gt[0:N_HEADS] - bcs, bcs, b16.T

    def project(hnb):
        ztm = _dot(hnb, wtm_s[...])
        kt_all = _dot_nt(wkt_s[...], hnb)
        return ztm, kt_all

    cns = [cn_s[h] for h in range(N_HEADS)]
    ms = [m_s[h] for h in range(N_HEADS)]
    hist = [hist_s[...]]
    last_ext = [None]

    def heads(ztm, kt_all, rr, bcs, bcs_col):
        outs = []
        for h in range(N_HEADS):
            sl = slice(h * HEAD_DIM, (h + 1) * HEAD_DIM)
            q = ztm[:, sl]
            v = ztm[:, MLSTM_WIDTH + h * HEAD_DIM:MLSTM_WIDTH + (h + 1) * HEAD_DIM]
            o = ztm[:, 2 * MLSTM_WIDTH + h * HEAD_DIM:2 * MLSTM_WIDTH + (h + 1) * HEAD_DIM]
            kt = kt_all[sl, :] * K_SCALE
            r_row = rr[h:h + 1, :]
            m0 = ms[h]
            cn = cns[h]

            rmat = jnp.where(causal, r_row, -jnp.inf)
            mcol = jnp.maximum(jnp.max(rmat, axis=-1, keepdims=True), m0)
            mcol_b = jnp.broadcast_to(mcol, (lc, HEAD_DIM))
            wmat = jnp.exp(rmat - jnp.concatenate([mcol_b] * lane_tiles, axis=1))
            a_b = jnp.exp(m0 - mcol_b)
            bcol = bcs_col[:, N_HEADS + h:N_HEADS + h + 1]

            qb = q.astype(BF16)
            vext = jnp.concatenate([v.astype(BF16), ones_blk], axis=1)
            s = _dot(qb, kt.astype(BF16)) * wmat
            qcn = _dot(qb, cn.astype(BF16))
            sv = _dot(s.astype(BF16), vext)
            nd = jnp.concatenate([a_b, a_b], axis=1) * qcn + sv
            num = nd[:, 0:HEAD_DIM]
            den = nd[:, HEAD_DIM:]
            hh = num / jnp.maximum(jnp.abs(den), jnp.exp(-(bcol + mcol)))
            hh = (_rms(hh) * ghead) * jax.nn.sigmoid(o)
            outs.append(hh.astype(BF16))

            ml = mcol[lc - 1:lc, :]
            al = jnp.exp(m0 - ml)
            wl = jnp.exp(r_row - ml)
            kw = (kt * wl).astype(BF16)
            cns[h] = al * cn + _dot(kw, vext)
            ms[h] = bcs[h:h + 1, lc - 1:lc] + ml
        return jnp.concatenate(outs, axis=1)

    def pool(c, u):
        ext = jnp.concatenate([hist[0], u], axis=0)
        p2 = ext + pltpu.roll(ext, 1, axis=0)
        p4 = p2[:, g:] + pltpu.roll(p2[:, g:], 2, axis=0)
        p8 = p4[:, g:] + pltpu.roll(p4[:, g:], 4, axis=0)
        p16 = p8[:, g:] + pltpu.roll(p8[:, g:], 8, axis=0)
        wsum = (p2[:, 0:g], p4[:, 0:g], p8[:, 0:g], p16)
        pos = (t * tl + c * lc + lax.broadcasted_iota(jnp.int32, (lc, 1), 0)).astype(F32)
        pouts = []
        for gi, win in enumerate(POOL_WINDOWS):
            cnt = jnp.minimum(pos + 1.0, float(win))
            ug = u[:, gi * g:(gi + 1) * g]
            pooled = wsum[gi][POOL_HIST:, :] / cnt - ug
            pouts.append(pooled.astype(BF16))
        hist[0] = ext[lc:, :]
        last_ext[0] = ext
        return jnp.concatenate(pouts, axis=1)

    def finish(c, mixh, p_out):
        mix = _dot(mixh, wout_s[0:MLSTM_WIDTH, :]) + _dot(p_out, wout_s[MLSTM_WIDTH:, :])
        x = x_ref[c * lc:(c + 1) * lc, :]
        x1_ref[c * lc:(c + 1) * lc, :] = x + ga1 * (_rms(mix) * gpost)

    pres = []
    zs = []
    z_smp = zs_ref[...]
    bt = z_smp.shape[0]
    smp_gates = z_smp[:, _CG:_CG + 2 * N_HEADS] + jnp.concatenate([sbig_ref[...], sbfg_ref[...]], axis=1)
    _, smp_w, smp_a = _sample_gates(smp_gates, sm0_ref[...])

    def sample_memory_update(h):
        sl = slice(h * HEAD_DIM, (h + 1) * HEAD_DIM)
        q_t = z_smp[:, _CQ + h * HEAD_DIM:_CQ + (h + 1) * HEAD_DIM].T
        kw_t = (z_smp[:, _CK + h * HEAD_DIM:_CK + (h + 1) * HEAD_DIM] * K_SCALE * smp_w[:, h:h + 1]).T
        rows = []
        for j in range(bt):
            mem = slice((j * N_HEADS + h) * HEAD_DIM, (j * N_HEADS + h + 1) * HEAD_DIM)
            c0 = sc0_ref[mem, :]
            v = z_smp[j:j + 1, _CV + h * HEAD_DIM:_CV + (h + 1) * HEAD_DIM]
            sc1_ref[mem, :] = smp_a[j:j + 1, h:h + 1] * c0 + kw_t[:, j:j + 1] * v
            rows.append(jnp.sum(q_t[:, j:j + 1] * c0, axis=0, keepdims=True))
        sqc_ref[:, sl] = jnp.concatenate(rows, axis=0)

    for c in range(n_chunks):
        pres.append(norm_and_gates(c))
        zs.append(project(pres[c][0]))
        for h in range(c, N_HEADS, n_chunks):
            sample_memory_update(h)
    mixhs = [heads(zs[c][0], zs[c][1], *pres[c][1:]) for c in range(n_chunks)]
    p_outs = [pool(c, zs[c][0][:, 3 * MLSTM_WIDTH:3 * MLSTM_WIDTH + POOL_WIDTH]) for c in range(n_chunks)]
    for c in range(n_chunks):
        finish(c, mixhs[c], p_outs[c])

    for h in range(N_HEADS):
        cn_s[h] = cns[h]
        m_s[h] = ms[h]
    hist_s[...] = hist[0]

    @pl.when(t == nt - 1)
    def _():
        for h in range(N_HEADS):
            c_ref[h] = cns[h][:, 0:HEAD_DIM]
            n_ref[h:h + 1, :] = cns[h][:, HEAD_DIM:].T[0:1, :]
            m_ref[h] = ms[h]
        pool_ref[...] = last_ext[0][lc + POOL_HIST - POOL_BUF:, :]


def _mix_prompt(x, mod_p, g_pre1, g_post1, w_in_t, b_ig, b_fg, g_head, w_pool, pool_scale, w_out, w_up, w_dn,
                z_s, m0_s, c0_s):
    bsz, seq, _ = x.shape
    n_s = z_s.shape[0]
    tl = MIX_TOKENS
    nt = seq // tl
    steps = bsz * nt
    assert D_MODEL % (16 * steps) == 0
    assert n_s % (8 * steps) == 0
    assert tl % STAGE_ROWS == 0 and MLSTM_WIDTH % STAGE_ROWS == 0
    up_rows = D_MODEL // steps
    dn_rows = D_FF // steps
    bt = n_s // steps
    const2 = lambda b, t: (0, 0)
    const3 = lambda b, t: (0, 0, 0)
    share = lambda b, t: (0, b * nt + t, 0)
    smem = pl.BlockSpec(memory_space=pltpu.SMEM)
    return pl.pallas_call(
        _mix_kernel,
        grid=(bsz, nt),
        in_specs=[
            pl.BlockSpec((None, tl, D_MODEL), lambda b, t: (b, t, 0)),
            pl.BlockSpec(mod_p.shape, const2),
            pl.BlockSpec((1, D_MODEL), const2),
            pl.BlockSpec((1, D_MODEL), const2),
            pl.BlockSpec(memory_space=pl.ANY),
            smem,
            smem,
            pl.BlockSpec((1, HEAD_DIM), const2),
            pl.BlockSpec(w_pool.shape, const3),
            pl.BlockSpec((1, POOL_WIDTH), const2),
            pl.BlockSpec(memory_space=pl.ANY),
            pl.BlockSpec((None, up_rows, D_FF), share),
            pl.BlockSpec((None, dn_rows, D_MODEL), share),
            pl.BlockSpec((bt, z_s.shape[1]), lambda b, t: (b * nt + t, 0)),
            pl.BlockSpec((bt, N_HEADS), lambda b, t: (b * nt + t, 0)),
            pl.BlockSpec((1, N_HEADS), const2),
            pl.BlockSpec((1, N_HEADS), const2),
            pl.BlockSpec((bt * N_HEADS * HEAD_DIM, HEAD_DIM), lambda b, t: (b * nt + t, 0)),
        ],
        out_specs=[
            pl.BlockSpec((tl, D_MODEL), lambda b, t: (b * nt + t, 0)),
            pl.BlockSpec((None, N_HEADS, HEAD_DIM, HEAD_DIM), lambda b, t: (b, 0, 0, 0)),
            pl.BlockSpec((None, N_HEADS, HEAD_DIM), lambda b, t: (b, 0, 0)),
            pl.BlockSpec((None, N_HEADS, 1, 1), lambda b, t: (b, 0, 0, 0)),
            pl.BlockSpec((None, POOL_BUF, POOL_WIDTH), lambda b, t: (b, 0, 0)),
            pl.BlockSpec((up_rows, D_FF), lambda b, t: (b * nt + t, 0)),
            pl.BlockSpec((dn_rows, D_MODEL), lambda b, t: (b * nt + t, 0)),
            pl.BlockSpec((bt * N_HEADS * HEAD_DIM, HEAD_DIM), lambda b, t: (b * nt + t, 0)),
            pl.BlockSpec((bt, MLSTM_WIDTH), lambda b, t: (b * nt + t, 0)),
        ],
        out_shape=[
            jax.ShapeDtypeStruct((bsz * seq, D_MODEL), F32),
            jax.ShapeDtypeStruct((bsz, N_HEADS, HEAD_DIM, HEAD_DIM), F32),
            jax.ShapeDtypeStruct((bsz, N_HEADS, HEAD_DIM), F32),
            jax.ShapeDtypeStruct((bsz, N_HEADS, 1, 1), F32),
            jax.ShapeDtypeStruct((bsz, POOL_BUF, POOL_WIDTH), F32),
            jax.ShapeDtypeStruct((D_MODEL, D_FF), BF16),
            jax.ShapeDtypeStruct((D_FF, D_MODEL), BF16),
            jax.ShapeDtypeStruct(c0_s.shape, F32),
            jax.ShapeDtypeStruct((n_s, MLSTM_WIDTH), F32),
        ],
        scratch_shapes=[
            pltpu.VMEM((D_MODEL, 3 * MLSTM_WIDTH + POOL_WIDTH), BF16),
            pltpu.VMEM((MLSTM_WIDTH, D_MODEL), BF16),
            pltpu.VMEM((GATE_ROWS, D_MODEL), BF16),
            pltpu.VMEM((D_MODEL, D_MODEL), BF16),
            pltpu.SemaphoreType.DMA((tl // STAGE_ROWS,)),
            pltpu.VMEM((N_HEADS, HEAD_DIM, 2 * HEAD_DIM), F32),
            pltpu.VMEM((N_HEADS, 1, 1), F32),
            pltpu.VMEM((POOL_HIST, POOL_WIDTH), F32),
        ],
        compiler_params=pltpu.CompilerParams(
            dimension_semantics=("arbitrary", "arbitrary"), vmem_limit_bytes=VMEM_LIMIT),
        name="mix_prompt",
    )(x, mod_p, g_pre1, g_post1, w_in_t, b_ig, b_fg, g_head, w_pool, pool_scale, w_out, w_up, w_dn,
      z_s, m0_s, b_ig, b_fg, c0_s)


def _mlp_rows(x_ref, y_ref, mod, gpre, gpost, wup_s, wdn_s, sub):
    sh2 = mod[:, 3 * D_MODEL:4 * D_MODEL]
    sc2 = mod[:, 4 * D_MODEL:5 * D_MODEL]
    ga2 = mod[:, 5 * D_MODEL:6 * D_MODEL]
    n_sub = x_ref.shape[0] // sub
    per_row = mod.shape[0] > 1

    def rows(r, a):
        return a[r * sub:(r + 1) * sub] if per_row else a

    hbs = []
    for r in range(n_sub):
        x = x_ref[r * sub:(r + 1) * sub, :]
        hbs.append(((_rms(x) * gpre) * (1.0 + rows(r, sc2)) + rows(r, sh2)).astype(BF16))
    accs = [jnp.zeros((sub, D_MODEL), F32) for _ in range(n_sub)]
    n_ff = D_FF // FF_CHUNK
    for j in range(n_ff):
        for r in range(n_sub):
            f = _dot(hbs[r], wup_s[:, j * FF_CHUNK:(j + 1) * FF_CHUNK])
            f = jnp.square(jnp.maximum(f, 0.0))
            accs[r] = accs[r] + _dot(f.astype(BF16), wdn_s[j * FF_CHUNK:(j + 1) * FF_CHUNK, :])
            if j == n_ff - 1:
                x = x_ref[r * sub:(r + 1) * sub, :]
                y = x + rows(r, ga2) * (_rms(accs[r]) * gpost)
                if len(y_ref.shape) == 3:
                    y_ref[r * sub:(r + 1) * sub, 0, :] = y
                else:
                    y_ref[r * sub:(r + 1) * sub, :] = y


def _mlp_kernel(xp_ref, xs_ref, modp_ref, mods_ref, gpre_ref, gpost_ref, wup_ref, wdn_ref,
                yp_ref, ys_ref, *, steps_per_mod):
    i = pl.program_id(0)
    n_prompt = pl.num_programs(0) - 1

    @pl.when(i < n_prompt)
    def _():
        mod = modp_ref[pl.ds(i // steps_per_mod, 1), :]
        _mlp_rows(xp_ref, yp_ref, mod, gpre_ref[...], gpost_ref[...], wup_ref, wdn_ref, MLP_SUB)

    @pl.when(i == n_prompt)
    def _():
        _mlp_rows(xs_ref, ys_ref, mods_ref[...], gpre_ref[...], gpost_ref[...], wup_ref, wdn_ref, xs_ref.shape[0])


def _mlp(xp, xs, mod_p, mod_s, rows_per_mod, g_pre2, g_post2, w_up_b, w_dn_b):
    n_p = xp.shape[0]
    n_s = xs.shape[0]
    tm = MLP_ROWS
    n_tiles = n_p // tm
    last = n_tiles - 1
    const2 = lambda i: (0, 0)
    ptile = lambda i: (jnp.minimum(i, last), 0)
    once = pl.Buffered(1)
    return pl.pallas_call(
        functools.partial(_mlp_kernel, steps_per_mod=rows_per_mod // tm),
        grid=(n_tiles + 1,),
        in_specs=[
            pl.BlockSpec((tm, D_MODEL), ptile),
            pl.BlockSpec((n_s, D_MODEL), const2, pipeline_mode=once),
            pl.BlockSpec(mod_p.shape, const2, pipeline_mode=once),
            pl.BlockSpec((n_s, 6 * D_MODEL), const2, pipeline_mode=once),
            pl.BlockSpec((1, D_MODEL), const2),
            pl.BlockSpec((1, D_MODEL), const2),
            pl.BlockSpec(w_up_b.shape, const2, pipeline_mode=once),
            pl.BlockSpec(w_dn_b.shape, const2, pipeline_mode=once),
        ],
        out_specs=[
            pl.BlockSpec((tm, D_MODEL), ptile),
            pl.BlockSpec((n_s, 1, D_MODEL), lambda i: (0, 0, 0)),
        ],
        out_shape=[
            jax.ShapeDtypeStruct((n_p, D_MODEL), F32),
            jax.ShapeDtypeStruct((n_s, 1, D_MODEL), F32),
        ],
        compiler_params=pltpu.CompilerParams(
            dimension_semantics=("arbitrary",), vmem_limit_bytes=VMEM_LIMIT),
        name="mlp",
    )(xp, xs, mod_p, mod_s, g_pre2, g_post2, w_up_b, w_dn_b)


def _sproj_kernel(x_ref, mod_ref, gpre_ref, w_ref, z_ref):
    x = x_ref[:, 0, :]
    mod = mod_ref[...]
    sh1 = mod[:, 0:D_MODEL]
    sc1 = mod[:, D_MODEL:2 * D_MODEL]
    hn = (_rms(x) * gpre_ref[...]) * (1.0 + sc1) + sh1
    z_ref[...] = _dot_nt(hn.astype(BF16), w_ref[...].astype(BF16))


def _sample_proj(x, mod_s, g_pre1, w_in_t):
    n = x.shape[0]
    return pl.pallas_call(
        _sproj_kernel,
        out_shape=jax.ShapeDtypeStruct((n, IN_COLS), F32),
        compiler_params=pltpu.CompilerParams(vmem_limit_bytes=VMEM_LIMIT),
        name="sample_proj",
    )(x, mod_s, g_pre1, w_in_t)


def _sample_gates(gates, m0):
    ig = gates[:, 0:N_HEADS]
    logf = jax.nn.log_sigmoid(gates[:, N_HEADS:2 * N_HEADS])
    g = logf + m0
    m = jnp.maximum(g, ig)
    return m, jnp.exp(ig - m), jnp.exp(g - m)


def _smix_kernel(x_ref, z_ref, qc_ref, mod_ref, n0_ref, m0_ref, pool_ref, big_ref, bfg_ref, ghead_ref,
                 wpool_ref, pscale_ref, wout_ref, gpost_ref,
                 x1_ref, n1_ref, m1_ref, poolo_ref):
    x = x_ref[:, 0, :]
    z = z_ref[...]
    mod = mod_ref[...]
    ga1 = mod[:, 2 * D_MODEL:3 * D_MODEL]
    gates = z[:, _CG:_CG + 2 * N_HEADS] + jnp.concatenate([big_ref[...], bfg_ref[...]], axis=1)
    m, w, a = _sample_gates(gates, m0_ref[...])
    m1_ref[...] = m
    ghead = ghead_ref[...]
    heads = []
    for h in range(N_HEADS):
        sl = slice(h * HEAD_DIM, (h + 1) * HEAD_DIM)
        q = z[:, _CQ + h * HEAD_DIM:_CQ + (h + 1) * HEAD_DIM]
        k = z[:, _CK + h * HEAD_DIM:_CK + (h + 1) * HEAD_DIM] * K_SCALE
        v = z[:, _CV + h * HEAD_DIM:_CV + (h + 1) * HEAD_DIM]
        o = z[:, _CO + h * HEAD_DIM:_CO + (h + 1) * HEAD_DIM]
        n0 = n0_ref[:, h, :]
        ah = a[:, h:h + 1]
        wh = w[:, h:h + 1]
        s = jnp.sum(q * k, axis=-1, keepdims=True) * wh
        num = ah * qc_ref[:, sl] + s * v
        den = ah * jnp.sum(q * n0, axis=-1, keepdims=True) + s
        hh = num / jnp.maximum(jnp.abs(den), jnp.exp(-m[:, h:h + 1]))
        n1_ref[:, h, :] = ah * n0 + wh * k
        hh = (_rms(hh) * ghead) * jax.nn.sigmoid(o)
        heads.append(hh.astype(BF16))

    u = z[:, _CU:_CU + POOL_WIDTH]
    g = POOL_GROUP_DIM
    pouts = []
    for gi, win in enumerate(POOL_WINDOWS):
        ug = u[:, gi * g:(gi + 1) * g]
        wsum = ug
        for r in range(POOL_BUF - (win - 1), POOL_BUF):
            wsum = wsum + pool_ref[r, :, gi * g:(gi + 1) * g]
        cnt = min(PAST_LEN + 1.0, float(win))
        pooled = wsum / cnt - ug
        pouts.append(_dot(pooled.astype(BF16), wpool_ref[gi].astype(BF16)))
    p_out = jnp.concatenate(pouts, axis=1) * pscale_ref[...]
    poolo_ref[0:POOL_BUF - 1] = pool_ref[1:POOL_BUF]
    poolo_ref[POOL_BUF - 1] = u

    mixh = jnp.concatenate(heads, axis=1)
    mix = (_dot(mixh, wout_ref[0:MLSTM_WIDTH, :].astype(BF16))
           + _dot(p_out.astype(BF16), wout_ref[MLSTM_WIDTH:, :].astype(BF16)))
    x1_ref[...] = x + ga1 * (_rms(mix) * gpost_ref[...])


def _sample_mix(x, z_s, qc, mod_s, n0, m0, pool_rows, b_ig, b_fg, g_head, w_pool, pool_scale, w_out, g_post1):
    n = x.shape[0]
    return pl.pallas_call(
        _smix_kernel,
        out_shape=[
            jax.ShapeDtypeStruct((n, D_MODEL), F32),
            jax.ShapeDtypeStruct((n, N_HEADS, HEAD_DIM), F32),
            jax.ShapeDtypeStruct((n, N_HEADS), F32),
            jax.ShapeDtypeStruct(pool_rows.shape, F32),
        ],
        compiler_params=pltpu.CompilerParams(vmem_limit_bytes=VMEM_LIMIT),
        name="sample_mix",
    )(x, z_s, qc, mod_s, n0, m0, pool_rows, b_ig, b_fg, g_head, w_pool, pool_scale, w_out, g_post1)


def kernel(x_prompt, x_sample, c_prompt, c_sample, state_C, state_n, state_m, state_pool, w_ada, b_ada,
           g_pre1, g_post1, w_in, b_ig, b_fg, g_head, w_pool, pool_scale, w_out, g_pre2, g_post2,
           w_up, w_down):
    depth = w_ada.shape[0]
    assert depth == 1, "single-layer step"
    bsz, seq, _ = x_prompt.shape
    nb = x_sample.shape[0]
    assert x_sample.shape[1] == 1 and seq % MIX_TOKENS == 0 and seq % MLP_ROWS == 0
    assert w_in.shape[2] == IN_COLS
    l = 0

    gpre1 = g_pre1[l][None, :]
    gpost1 = g_post1[l][None, :]
    gpre2 = g_pre2[l][None, :]
    gpost2 = g_post2[l][None, :]
    ghead = g_head[l][None, :]
    pscale = pool_scale[l][None, :]
    w_in_t = jnp.transpose(w_in[l])

    mod_p, mod_s = _ada(c_prompt, c_sample, w_ada[l], b_ada[l][None, :])

    m0 = state_m[l]
    z_s = _sample_proj(x_sample, mod_s, gpre1, w_in_t)

    x1p, c_p, n_p, m_p, pool_p, w_up_b, w_dn_b, c_s, qc = _mix_prompt(
        x_prompt, mod_p, gpre1, gpost1, w_in_t, b_ig, b_fg, ghead, w_pool[l], pscale, w_out, w_up, w_down,
        z_s, m0, state_C[l].reshape(nb * N_HEADS * HEAD_DIM, HEAD_DIM))

    x1s, n_s, m_s, pool_s = _sample_mix(
        x_sample, z_s, qc, mod_s, state_n[l], m0,
        jnp.transpose(state_pool[l], (1, 0, 2)), b_ig, b_fg, ghead, w_pool[l], pscale,
        w_out[l], gpost1)

    y_p, y_s = _mlp(x1p, x1s, mod_p, mod_s, seq, gpre2, gpost2, w_up_b, w_dn_b)

    return (y_p.reshape(bsz, seq, D_MODEL), y_s,
            c_p[None], n_p[None], m_p.reshape(1, bsz, N_HEADS), pool_p[None],
            c_s.reshape(state_C.shape), n_s[None], m_s[None],
            jnp.transpose(pool_s, (1, 0, 2))[None])
```
